```python
import jax
import jax.numpy as jnp
from jax import lax
import numpy as np

D_MODEL = 1024
BATCH = 8
SEQ = 2048
DEPTH = 1

MLA_HEADS = 8
MLA_Q_RANK = 256
MLA_KV_RANK = 128
MLA_NOPE = 64
MLA_ROPE = 32
MLA_V = 64
ROPE_THETA = 10000.0
DSA_HEADS = 8
DSA_KV_HEADS = 2
DSA_HEAD_DIM = 64
IDX_HEADS = 8
IDX_DIM = 32
IDX_TOPK_MAX = 256
N_EXPERTS = 32
TOP_K = 4
D_EXPERT = D_MODEL
SWIGLU_ALPHA = 1.702
SWIGLU_LIMIT = 7.0

NORM_EPS = 1e-6
NEG_INF = -1e30
Q_BLOCK = 128
MOE_BLOCK = 128

MIX_WIDTH = MLA_HEADS * MLA_V + DSA_HEADS * DSA_HEAD_DIM
IN_SPLITS = (MLA_Q_RANK, MLA_KV_RANK, MLA_ROPE,
             DSA_HEADS * DSA_HEAD_DIM, DSA_KV_HEADS * DSA_HEAD_DIM, DSA_KV_HEADS * DSA_HEAD_DIM,
             IDX_HEADS * IDX_DIM, IDX_DIM, IDX_HEADS)
IN_WIDTH = sum(IN_SPLITS)

kernel_name = "hybrid_mla_dsa_moe_block"


def rms_norm(x, g):
    xf = x.astype(jnp.float32)
    y = xf * lax.rsqrt(jnp.mean(xf * xf, axis=-1, keepdims=True) + NORM_EPS)
    return (y * g.astype(jnp.float32)).astype(x.dtype)


def rope(x, pos):
    half = x.shape[-1] // 2
    freqs = ROPE_THETA ** (-jnp.arange(half, dtype=jnp.float32) / half)
    ang = pos.astype(jnp.float32)[..., None] * freqs
    ang = ang.reshape(ang.shape[:2] + (1,) * (x.ndim - 3) + (half,))
    cos, sin = jnp.cos(ang), jnp.sin(ang)
    x1 = x[..., :half].astype(jnp.float32)
    x2 = x[..., half:].astype(jnp.float32)
    return jnp.concatenate([x1 * cos - x2 * sin, x2 * cos + x1 * sin], axis=-1).astype(x.dtype)


def to_blocks(a):
    b, s = a.shape[:2]
    return jnp.moveaxis(a.reshape((b, s // Q_BLOCK, Q_BLOCK) + a.shape[2:]), 1, 0)


def from_blocks(a):
    a = jnp.moveaxis(a, 0, 1)
    return a.reshape((a.shape[0], a.shape[1] * a.shape[2]) + a.shape[3:])


def mla_attention(c_q, c_kv, k_rope_in, pos, g_q_a, g_kv_a, w_q_b, w_kv_b):
    b, s, _ = c_q.shape
    q = (rms_norm(c_q, g_q_a) @ w_q_b).reshape(b, s, MLA_HEADS, MLA_NOPE + MLA_ROPE)
    q_nope = q[..., :MLA_NOPE]
    q_rope = rope(q[..., MLA_NOPE:], pos)
    kv = (rms_norm(c_kv, g_kv_a) @ w_kv_b).reshape(b, s, MLA_HEADS, MLA_NOPE + MLA_V)
    k_nope, v = kv[..., :MLA_NOPE], kv[..., MLA_NOPE:]
    k_rope = rope(k_rope_in, pos)
    scale = (MLA_NOPE + MLA_ROPE) ** -0.5
    key_idx = jnp.arange(s)

    def block(args):
        qn, qr, qi = args
        sc = (jnp.einsum('bqhd,bkhd->bhqk', qn, k_nope)
              + jnp.einsum('bqhd,bkd->bhqk', qr, k_rope)).astype(jnp.float32) * scale
        sc = jnp.where(key_idx[None, None, None, :] <= qi[None, None, :, None], sc, NEG_INF)
        p = jax.nn.softmax(sc, axis=-1).astype(v.dtype)
        return jnp.einsum('bhqk,bkhd->bqhd', p, v)

    q_ids = jnp.arange(s).reshape(-1, Q_BLOCK)
    out = lax.map(block, (to_blocks(q_nope), to_blocks(q_rope), q_ids))
    return from_blocks(out).reshape(b, s, MLA_HEADS * MLA_V)


def dsa_attention(q, k, v, q_i, k_i, w_i, pos):
    b, s, _ = q.shape
    grp = DSA_HEADS // DSA_KV_HEADS
    q = q.reshape(b, s, DSA_KV_HEADS, grp, DSA_HEAD_DIM)
    k = k.reshape(b, s, DSA_KV_HEADS, DSA_HEAD_DIM)
    v = v.reshape(b, s, DSA_KV_HEADS, DSA_HEAD_DIM)
    q_i = q_i.reshape(b, s, IDX_HEADS, IDX_DIM) * (IDX_DIM ** -0.5)
    w_i = w_i * (IDX_HEADS ** -0.5)
    n_sel = min(IDX_TOPK_MAX, s // 4)
    key_idx = jnp.arange(s)
    slopes = jnp.exp2(-8.0 * jnp.arange(1, DSA_HEADS + 1, dtype=jnp.float32) / DSA_HEADS)
    slopes = slopes.reshape(DSA_KV_HEADS, grp)
    gather = jax.vmap(lambda a, i: a[i])

    def block(args):
        qb, qib, wib, posb, ti = args
        rel = jax.nn.relu(jnp.einsum('bqhd,bsd->bqsh', qib, k_i).astype(jnp.float32))
        idx_score = jnp.einsum('bqsh,bqh->bqs', rel, wib.astype(jnp.float32))
        idx_score = jnp.where(key_idx[None, None, :] <= ti[None, :, None], idx_score, NEG_INF)
        _, sel = lax.top_k(idx_score, n_sel)
        k_sel = gather(k, sel)
        v_sel = gather(v, sel)
        pos_sel = gather(pos, sel)
        sc = jnp.einsum('bqhgd,bqnhd->bqhgn', qb, k_sel).astype(jnp.float32) * (DSA_HEAD_DIM ** -0.5)
        dist = jnp.abs(posb[:, :, None] - pos_sel).astype(jnp.float32)
        sc = sc - slopes[None, None, :, :, None] * dist[:, :, None, None, :]
        valid = sel <= ti[None, :, None]
        sc = jnp.where(valid[:, :, None, None, :], sc, NEG_INF)
        p = jax.nn.softmax(sc, axis=-1).astype(v.dtype)
        return jnp.einsum('bqhgn,bqnhd->bqhgd', p, v_sel)

    q_ids = jnp.arange(s).reshape(-1, Q_BLOCK)
    out = lax.map(block, (to_blocks(q), to_blocks(q_i), to_blocks(w_i), to_blocks(pos), q_ids))
    return from_blocks(out).reshape(b, s, DSA_HEADS * DSA_HEAD_DIM)


def moe_ffn(h, w_router, b_router, w_mlp1, b_mlp1, w_mlp2, b_mlp2):
    b, s, d = h.shape
    xf = h.reshape(-1, d)
    n = xf.shape[0]
    logits = (xf @ w_router + b_router).astype(jnp.float32)
    top_val, top_idx = lax.top_k(logits, TOP_K)
    gates = jax.nn.softmax(top_val, axis=-1)
    n_assign = n * TOP_K
    flat_e = top_idx.reshape(-1)
    flat_tok = jnp.arange(n_assign, dtype=jnp.int32) // TOP_K
    flat_gate = gates.reshape(-1)
    order = jnp.argsort(flat_e)
    e_sorted = flat_e[order]
    counts = jnp.bincount(flat_e, length=N_EXPERTS)
    start = jnp.cumsum(counts) - counts
    padded = (counts + MOE_BLOCK - 1) // MOE_BLOCK * MOE_BLOCK
    pad_end = jnp.cumsum(padded)
    pad_start = pad_end - padded
    dest = pad_start[e_sorted] + jnp.arange(n_assign) - start[e_sorted]
    n_slots = (n_assign + MOE_BLOCK - 1) // MOE_BLOCK * MOE_BLOCK + N_EXPERTS * MOE_BLOCK
    n_blocks = n_slots // MOE_BLOCK
    slot_tok = jnp.full((n_slots,), n, jnp.int32).at[dest].set(flat_tok[order])
    slot_gate = jnp.zeros((n_slots,), jnp.float32).at[dest].set(flat_gate[order])
    block_e = jnp.minimum(
        jnp.searchsorted(pad_end, jnp.arange(n_blocks) * MOE_BLOCK, side='right'), N_EXPERTS - 1)
    x_pad = jnp.concatenate([xf, jnp.zeros((1, d), xf.dtype)], axis=0)
    x_slots = x_pad[slot_tok].reshape(n_blocks, MOE_BLOCK, d)

    def expert_block(args):
        xb, e = args
        hid = xb @ w_mlp1[e] + b_mlp1[e]
        glu = jnp.minimum(hid[..., ::2], SWIGLU_LIMIT)
        lin = jnp.clip(hid[..., 1::2], -SWIGLU_LIMIT, SWIGLU_LIMIT)
        act = glu * jax.nn.sigmoid(SWIGLU_ALPHA * glu) * (lin + 1.0)
        return act @ w_mlp2[e] + b_mlp2[e]

    y = lax.map(expert_block, (x_slots, block_e)).reshape(n_slots, d)
    y = y * slot_gate[:, None].astype(y.dtype)
    out = jax.ops.segment_sum(y, slot_tok, num_segments=n + 1)[:n]
    return out.reshape(b, s, d)


def setup_inputs(seed: int = 0) -> dict:
    key = jax.random.key(seed)
    ks = jax.random.split(key, 24)
    f32 = jnp.float32
    L, D, F, E = DEPTH, D_MODEL, D_EXPERT, N_EXPERTS

    def nrm(k, shape, scale):
        return jax.random.normal(k, shape, f32) * scale

    def gain(k, width):
        return 1.0 + nrm(k, (L, width), 0.02)

    x = nrm(ks[0], (BATCH, SEQ, D), 1.0)
    c = nrm(ks[1], (BATCH, D), 1.0)
    offsets = jax.random.randint(ks[2], (BATCH, 1), 0, 4096, dtype=jnp.int32)
    positions = offsets + jnp.arange(SEQ, dtype=jnp.int32)[None, :]
    return {
        "x": x,
        "c": c,
        "positions": positions,
        "w_ada": nrm(ks[3], (L, D, 6 * D), 0.2 * D ** -0.5),
        "b_ada": nrm(ks[4], (L, 6 * D), 0.01),
        "g_pre_mix": gain(ks[5], D),
        "g_post_mix": gain(ks[6], D),
        "g_pre_ffn": gain(ks[7], D),
        "g_post_ffn": gain(ks[8], D),
        "w_in": nrm(ks[9], (L, D, IN_WIDTH), D ** -0.5),
        "g_q_a": gain(ks[10], MLA_Q_RANK),
        "g_kv_a": gain(ks[11], MLA_KV_RANK),
        "w_q_b": nrm(ks[12], (L, MLA_Q_RANK, MLA_HEADS * (MLA_NOPE + MLA_ROPE)), MLA_Q_RANK ** -0.5),
        "w_kv_b": nrm(ks[13], (L, MLA_KV_RANK, MLA_HEADS * (MLA_NOPE + MLA_V)), MLA_KV_RANK ** -0.5),
        "w_o": nrm(ks[14], (L, MIX_WIDTH, D), MIX_WIDTH ** -0.5),
        "w_router": nrm(ks[15], (L, D, E), D ** -0.5),
        "b_router": nrm(ks[16], (L, E), 0.01),
        "w_mlp1": nrm(ks[17], (L, E, D, 2 * F), D ** -0.5),
        "b_mlp1": nrm(ks[18], (L, E, 2 * F), 0.01),
        "w_mlp2": nrm(ks[19], (L, E, F, D), F ** -0.5),
        "b_mlp2": nrm(ks[20], (L, E, D), 0.01),
    }


def reference(x, c, positions, w_ada, b_ada, g_pre_mix, g_post_mix, g_pre_ffn, g_post_ffn,
              w_in, g_q_a, g_kv_a, w_q_b, w_kv_b, w_o, w_router, b_router,
              w_mlp1, b_mlp1, w_mlp2, b_mlp2):
    split_at = np.cumsum(IN_SPLITS)[:-1].tolist()
    cond = jax.nn.silu(c)
    for l in range(DEPTH):
        mod = cond @ w_ada[l] + b_ada[l]
        sh1, sc1, gt1, sh2, sc2, gt2 = [m[:, None, :] for m in jnp.split(mod, 6, axis=-1)]

        h = rms_norm(x, g_pre_mix[l]) * (1.0 + sc1) + sh1
        proj = h @ w_in[l]
        (c_q, c_kv, k_rope, q_d, k_d, v_d, q_idx, k_idx, w_idx) = jnp.split(proj, split_at, axis=-1)
        y_mla = mla_attention(c_q, c_kv, k_rope, positions, g_q_a[l], g_kv_a[l], w_q_b[l], w_kv_b[l])
        y_dsa = dsa_attention(q_d, k_d, v_d, q_idx, k_idx, w_idx, positions)
        mix = jnp.concatenate([y_mla, y_dsa], axis=-1) @ w_o[l]
        x = x + gt1 * rms_norm(mix, g_post_mix[l])

        h2 = rms_norm(x, g_pre_ffn[l]) * (1.0 + sc2) + sh2
        ffn = moe_ffn(h2, w_router[l], b_router[l], w_mlp1[l], b_mlp1[l], w_mlp2[l], b_mlp2[l])
        x = x + gt2 * rms_norm(ffn, g_post_ffn[l])
    return x
```

```python
import functools

import jax
import jax.numpy as jnp
from jax import lax
from jax.experimental import pallas as pl
from jax.experimental.pallas import tpu as pltpu

F32 = jnp.float32
BF16 = jnp.bfloat16
I32 = jnp.int32

MLA_HEADS = 8
MLA_Q_RANK = 256
MLA_KV_RANK = 128
MLA_NOPE = 64
MLA_ROPE = 32
MLA_V = 64
ROPE_THETA = 10000.0
DSA_HEADS = 8
DSA_KV_HEADS = 2
DSA_HEAD_DIM = 64
IDX_HEADS = 8
IDX_DIM = 32
IDX_TOPK_MAX = 256
N_EXPERTS = 32
TOP_K = 4
SWIGLU_ALPHA = 1.702
SWIGLU_LIMIT = 7.0
NORM_EPS = 1e-6
NEG_INF = -1e30

LANES = 128
HEAD_PAD = 128
VMEM_LIMIT = 52 * 1024 * 1024

TM_PROJ = 512
TQ_MLA = 256
QB_DSA = 128
CK_DSA = 256
TM_POST = 512
BM_FFN = 256
TM_ROWS = 256

_NT = (((1,), (1,)), ((), ()))


def _rms(x, g):
    ms = jnp.mean(x * x, axis=-1, keepdims=True)
    return x * lax.rsqrt(ms + NORM_EPS) * g


def _cparams(sem, vmem=VMEM_LIMIT):
    return pltpu.CompilerParams(dimension_semantics=sem, vmem_limit_bytes=vmem)


def _ada_kernel(c_ref, w_ref, b_ref, o_ref):
    c = c_ref[...]
    cond = c * jax.nn.sigmoid(c)
    o_ref[...] = jnp.dot(cond.astype(BF16), w_ref[...].astype(BF16),
                         preferred_element_type=F32) + b_ref[...]


def _ada(c, w_ada, b_ada):
    b, d = c.shape
    n = w_ada.shape[1]
    tn = 1024
    return pl.pallas_call(
        _ada_kernel,
        out_shape=jax.ShapeDtypeStruct((b, n), F32),
        grid=(n // tn,),
        in_specs=[pl.BlockSpec((b, d), lambda i: (0, 0)),
                  pl.BlockSpec((d, tn), lambda i: (0, i)),
                  pl.BlockSpec((1, tn), lambda i: (0, i))],
        out_specs=pl.BlockSpec((b, tn), lambda i: (0, i)),
        compiler_params=_cparams(("arbitrary",)),
        name="ada",
    )(c, w_ada, b_ada.reshape(1, n))


_O_CQ = 0
_O_CKV = _O_CQ + MLA_Q_RANK
_O_KRA = _O_CKV + MLA_KV_RANK
_O_KRB = _O_KRA + HEAD_PAD
_O_QD = _O_KRB + HEAD_PAD
_O_KD = _O_QD + DSA_HEADS * DSA_HEAD_DIM
_O_VD = _O_KD + DSA_KV_HEADS * DSA_HEAD_DIM
_O_QI = _O_VD + DSA_KV_HEADS * DSA_HEAD_DIM
_O_KIW = _O_QI + IDX_HEADS * IDX_DIM
_W_IN = _O_KIW + LANES


def _inproj_kernel(x_ref, mod_ref, pos_ref, gpre_ref, win_ref, gq_ref, gkv_ref, wq_ref, wkv_ref,
                   rope_ref, qm_ref, km_ref, vm_ref, qd_ref, kd_ref, vd_ref, qi_ref, kiw_ref):
    d = x_ref.shape[2]
    x = x_ref[0]
    sh1 = mod_ref[0, :, 0:d]
    sc1 = mod_ref[0, :, d:2 * d]
    h = _rms(x, gpre_ref[...]) * (1.0 + sc1) + sh1
    proj = jnp.dot(h.astype(BF16), win_ref[...], preferred_element_type=F32)

    ang = pos_ref[0].astype(F32) * rope_ref[0:1, :]
    cos_t = jnp.cos(ang) * rope_ref[1:2, :] + rope_ref[3:4, :]
    sin_t = jnp.sin(ang) * rope_ref[2:3, :]

    hw = MLA_HEADS * HEAD_PAD
    nq = _rms(proj[:, _O_CQ:_O_CQ + MLA_Q_RANK], gq_ref[...])
    qab = jnp.dot(nq.astype(BF16), wq_ref[...], preferred_element_type=F32)
    scale = (MLA_NOPE + MLA_ROPE) ** -0.5
    nkv = _rms(proj[:, _O_CKV:_O_CKV + MLA_KV_RANK], gkv_ref[...])
    kv = jnp.dot(nkv.astype(BF16), wkv_ref[...], preferred_element_type=F32)
    kr = proj[:, _O_KRA:_O_KRA + HEAD_PAD] * cos_t + proj[:, _O_KRB:_O_KRB + HEAD_PAD] * sin_t
    for hd in range(MLA_HEADS):
        sl = slice(hd * HEAD_PAD, (hd + 1) * HEAD_PAD)
        slb = slice(hw + hd * HEAD_PAD, hw + (hd + 1) * HEAD_PAD)
        qm_ref[0, :, sl] = ((qab[:, sl] * cos_t + qab[:, slb] * sin_t) * scale).astype(BF16)
        km_ref[0, :, sl] = (kv[:, sl] + kr).astype(BF16)
    vm_ref[0] = kv[:, hw:hw + MLA_HEADS * MLA_V].astype(BF16)

    qd_ref[0] = (proj[:, _O_QD:_O_KD] * (DSA_HEAD_DIM ** -0.5)).astype(BF16)
    kd_ref[0] = proj[:, _O_KD:_O_VD].astype(BF16)
    vd_ref[0] = proj[:, _O_VD:_O_QI].astype(BF16)
    qi_ref[0] = (proj[:, _O_QI:_O_KIW] * (IDX_DIM ** -0.5)).astype(BF16)
    kiw_ref[0] = proj[:, _O_KIW:_W_IN]


def _relayout_in_weights(w_in, w_q_b, w_kv_b):
    d = w_in.shape[0]
    half = MLA_ROPE // 2
    o = 0
    segs = {}
    for name, width in (("cq", MLA_Q_RANK), ("ckv", MLA_KV_RANK), ("kr", MLA_ROPE),
                        ("qd", DSA_HEADS * DSA_HEAD_DIM), ("kd", DSA_KV_HEADS * DSA_HEAD_DIM),
                        ("vd", DSA_KV_HEADS * DSA_HEAD_DIM), ("qi", IDX_HEADS * IDX_DIM),
                        ("ki", IDX_DIM), ("wi", IDX_HEADS)):
        segs[name] = w_in[:, o:o + width]
        o += width
    z = lambda n: jnp.zeros((d, n), w_in.dtype)
    x1, x2 = segs["kr"][:, :half], segs["kr"][:, half:]
    tail = HEAD_PAD - MLA_NOPE - MLA_ROPE
    kra = jnp.concatenate([z(MLA_NOPE), x1, x2, z(tail)], axis=1)
    krb = jnp.concatenate([z(MLA_NOPE), x2, x1, z(tail)], axis=1)
    win = jnp.concatenate([segs["cq"], segs["ckv"], kra, krb, segs["qd"], segs["kd"], segs["vd"],
                           segs["qi"], segs["ki"], segs["wi"], z(LANES - IDX_DIM - IDX_HEADS)], axis=1)
    r = w_q_b.shape[0]
    wq = w_q_b.reshape(r, MLA_HEADS, MLA_NOPE + MLA_ROPE)
    zq = lambda n: jnp.zeros((r, MLA_HEADS, n), w_q_b.dtype)
    wqa = jnp.concatenate([wq, zq(tail)], axis=2).reshape(r, MLA_HEADS * HEAD_PAD)
    wqb = jnp.concatenate([zq(MLA_NOPE), wq[:, :, MLA_NOPE + half:], wq[:, :, MLA_NOPE:MLA_NOPE + half],
                           zq(tail)], axis=2).reshape(r, MLA_HEADS * HEAD_PAD)
    rk = w_kv_b.shape[0]
    wkv = w_kv_b.reshape(rk, MLA_HEADS, MLA_NOPE + MLA_V)
    wkn = jnp.concatenate([wkv[:, :, :MLA_NOPE], jnp.zeros((rk, MLA_HEADS, HEAD_PAD - MLA_NOPE), w_kv_b.dtype)],
                          axis=2).reshape(rk, MLA_HEADS * HEAD_PAD)
    wv = wkv[:, :, MLA_NOPE:].reshape(rk, MLA_HEADS * MLA_V)
    return (win.astype(BF16), jnp.concatenate([wqa, wqb], axis=1).astype(BF16),
            jnp.concatenate([wkn, wv], axis=1).astype(BF16))


def _rope_rows():
    half = MLA_ROPE // 2
    lane = jnp.arange(LANES)
    freqs = ROPE_THETA ** (-jnp.arange(half, dtype=F32) / half)
    in_x1 = (lane >= MLA_NOPE) & (lane < MLA_NOPE + half)
    in_x2 = (lane >= MLA_NOPE + half) & (lane < MLA_NOPE + MLA_ROPE)
    fr = jnp.where(in_x1 | in_x2, freqs[(lane - MLA_NOPE) % half], 0.0)
    cosm = (in_x1 | in_x2).astype(F32)
    sinm = jnp.where(in_x1, -1.0, jnp.where(in_x2, 1.0, 0.0))
    nopem = (lane < MLA_NOPE).astype(F32)
    rows = jnp.stack([fr, cosm, sinm, nopem], axis=0).astype(F32)
    return jnp.concatenate([rows, jnp.zeros((4, LANES), F32)], axis=0)


def _inproj(x, mod3, pos3, g_pre, win, g_q, g_kv, wq, wkv, rope_rows):
    b, s, d = x.shape
    tm = TM_PROJ
    tok = lambda w: pl.BlockSpec((1, tm, w), lambda bi, i: (bi, i, 0))
    full = lambda a: pl.BlockSpec(a.shape, lambda bi, i: (0,) * a.ndim)
    widths = (MLA_HEADS * HEAD_PAD, MLA_HEADS * HEAD_PAD, MLA_HEADS * MLA_V,
              DSA_HEADS * DSA_HEAD_DIM, DSA_KV_HEADS * DSA_HEAD_DIM, DSA_KV_HEADS * DSA_HEAD_DIM,
              IDX_HEADS * IDX_DIM, LANES)
    dts = (BF16,) * 7 + (F32,)
    return pl.pallas_call(
        _inproj_kernel,
        out_shape=[jax.ShapeDtypeStruct((b, s, w), dt) for w, dt in zip(widths, dts)],
        grid=(b, s // tm),
        in_specs=[tok(d),
                  pl.BlockSpec((1, 1, mod3.shape[2]), lambda bi, i: (bi, 0, 0)),
                  tok(1), full(g_pre), full(win), full(g_q), full(g_kv), full(wq), full(wkv),
                  full(rope_rows)],
        out_specs=[tok(w) for w in widths],
        compiler_params=_cparams(("parallel", "arbitrary")),
        name="inproj",
    )(x, mod3, pos3, g_pre, win, g_q, g_kv, wq, wkv, rope_rows)


def _mla_kernel(q_ref, k_ref, v_ref, o_ref):
    j = pl.program_id(1)
    tq = q_ref.shape[1]
    tk = tq
    row = lax.broadcasted_iota(I32, (tq, tk), 0)
    col = lax.broadcasted_iota(I32, (tq, tk), 1)
    lane = lax.broadcasted_iota(I32, (tq, 2 * MLA_V), 1)
    for hp in range(MLA_HEADS // 2):
        outs = []
        for hh in range(2):
            hd = hp * 2 + hh
            q = q_ref[0, :, hd * HEAD_PAD:(hd + 1) * HEAD_PAD]

            def chunk(c, carry, diagonal, hd=hd, hp=hp, q=q):
                m, l, acc = carry
                off = pl.multiple_of(c * tk, tk)
                k = k_ref[0, pl.ds(off, tk), hd * HEAD_PAD:(hd + 1) * HEAD_PAD]
                v = v_ref[0, pl.ds(off, tk), hp * 2 * MLA_V:(hp + 1) * 2 * MLA_V]
                s = lax.dot_general(q, k, _NT, preferred_element_type=F32)
                if diagonal:
                    s = jnp.where(col <= row, s, NEG_INF)
                m_new = jnp.maximum(m, jnp.max(s, axis=-1, keepdims=True))
                alpha = jnp.exp(m - m_new)
                p = jnp.exp(s - m_new)
                l = alpha * l + jnp.sum(p, axis=-1, keepdims=True)
                acc = alpha * acc + jnp.dot(p.astype(BF16), v, preferred_element_type=F32)
                return m_new, l, acc

            init = (jnp.full((tq, 1), NEG_INF, F32), jnp.zeros((tq, 1), F32),
                    jnp.zeros((tq, 2 * MLA_V), F32))
            carry = lax.fori_loop(0, j, functools.partial(chunk, diagonal=False), init)
            _, l, acc = chunk(j, carry, diagonal=True)
            outs.append(acc / l)
        o = jnp.where(lane < MLA_V, outs[0], outs[1])
        o_ref[0, :, hp * 2 * MLA_V:(hp + 1) * 2 * MLA_V] = o.astype(BF16)


def _mla(qm, km, vm):
    b, s, hw = qm.shape
    tq = TQ_MLA
    vw = vm.shape[2]
    return pl.pallas_call(
        _mla_kernel,
        out_shape=jax.ShapeDtypeStruct((b, s, vw), BF16),
        grid=(b, s // tq),
        in_specs=[pl.BlockSpec((1, tq, hw), lambda bi, j: (bi, j, 0)),
                  pl.BlockSpec((1, s, hw), lambda bi, j: (bi, 0, 0)),
                  pl.BlockSpec((1, s, vw), lambda bi, j: (bi, 0, 0))],
        out_specs=pl.BlockSpec((1, tq, vw), lambda bi, j: (bi, j, 0)),
        compiler_params=_cparams(("parallel", "arbitrary")),
        name="mla",
    )(qm, km, vm)


_KEY_NEG_INF = -2139095041
_KEY_POS_INF = 2139095040
_I32_MAX = 2147483647


def _key_to_f32(k):
    bits = k ^ ((k >> 31) & _I32_MAX)
    return lax.bitcast_convert_type(bits, F32)


def _dsa_kernel(qi_ref, kiw_ref, qd_ref, kd_ref, vd_ref, posq_ref, posk_ref, y_ref, sc_ref, mask_ref):
    j = pl.program_id(1)
    qb, ck = QB_DSA, CK_DSA
    sub = ck // qb
    nch = (j * qb + qb + ck - 1) // ck
    n_sel = IDX_TOPK_MAX

    kiw_q = kiw_ref[0, pl.ds(pl.multiple_of(j * qb, qb), qb), :]
    w_t = kiw_q.T[IDX_DIM:IDX_DIM + IDX_HEADS, :] * (IDX_HEADS ** -0.5)
    qi = qi_ref[0]
    qi_stack = jnp.concatenate([qi[:, hd * IDX_DIM:(hd + 1) * IDX_DIM] for hd in range(IDX_HEADS)], axis=0)
    q_idx = j * qb + lax.broadcasted_iota(I32, (1, qb), 1)
    k_iota = lax.broadcasted_iota(I32, (ck, 1), 0)

    def idx_chunk(c, carry):
        off = pl.multiple_of(c * ck, ck)
        ki = kiw_ref[0, pl.ds(off, ck), :][:, 0:IDX_DIM].astype(BF16)
        r = lax.dot_general(ki, qi_stack, _NT, preferred_element_type=F32)
        acc = jnp.zeros((ck, qb), F32)
        for hd in range(IDX_HEADS):
            acc = acc + w_t[hd:hd + 1, :] * jnp.maximum(r[:, hd * qb:(hd + 1) * qb], 0.0)
        sc_ref[pl.ds(off, ck), :] = jnp.where(off + k_iota <= q_idx, acc, NEG_INF)
        return carry

    lax.fori_loop(0, nch, idx_chunk, 0)

    def count(pred_fn):
        def body(c, cnt):
            off = pl.multiple_of(c * ck, ck)
            hit = pred_fn(sc_ref[pl.ds(off, ck), :], off).astype(F32)
            return cnt + jnp.sum(hit.reshape(ck // 8, 8, qb), axis=0)
        part = lax.fori_loop(0, nch, body, jnp.zeros((8, qb), F32))
        return jnp.sum(part, axis=0, keepdims=True)

    def select(_):
        def bisect(_, lohi):
            lo, hi = lohi
            mid = (lo & hi) + ((lo ^ hi) >> 1)
            t = _key_to_f32(mid)
            ok = count(lambda x, off: x >= t) >= n_sel
            return jnp.where(ok, mid, lo), jnp.where(ok, hi, mid)

        lo0 = jnp.full((1, qb), _KEY_NEG_INF, I32)
        hi0 = jnp.full((1, qb), _KEY_POS_INF, I32)
        lo, hi = lax.fori_loop(0, 32, bisect, (lo0, hi0))
        t_lo = _key_to_f32(lo)
        t_hi = _key_to_f32(lo + 1)
        c_gt = count(lambda x, off: x >= t_hi)
        c_ge = count(lambda x, off: x >= t_lo)
        need = n_sel - c_gt

        def tie_search(_):
            def step(_, lohi):
                mlo, mhi = lohi
                mid = (mlo + mhi) >> 1
                cnt = count(lambda x, off: (x >= t_lo) & jnp.logical_not(x >= t_hi)
                            & (off + k_iota <= mid))
                ok = cnt >= need
                return jnp.where(ok, mlo, mid), jnp.where(ok, mid, mhi)
            mlo0 = jnp.full((1, qb), -1, I32)
            mhi0 = jnp.full((1, qb), sc_ref.shape[0] - 1, I32)
            return lax.fori_loop(0, 12, step, (mlo0, mhi0))[1]

        tied = jnp.max(jnp.abs(c_ge - n_sel)) > 0.5
        m_sel = lax.cond(tied, tie_search, lambda _: jnp.full((1, qb), _I32_MAX, I32), 0)
        return t_lo, t_hi, m_sel

    def all_causal(_):
        return (jnp.full((1, qb), 0.1 * NEG_INF, F32), jnp.full((1, qb), jnp.inf, F32),
                jnp.full((1, qb), _I32_MAX, I32))

    t_lo, t_hi, m_sel = lax.cond((j + 1) * qb > n_sel, select, all_causal, 0)

    def mask_tile(i, carry):
        off = pl.multiple_of(i * qb, qb)
        x = sc_ref[pl.ds(off, qb), :]
        krow = off + lax.broadcasted_iota(I32, (qb, 1), 0)
        sel = (x >= t_hi) | ((x >= t_lo) & (krow <= m_sel))
        mask_ref[i] = sel.astype(F32).T
        return carry

    lax.fori_loop(0, nch * sub, mask_tile, 0)

    posq = posq_ref[0]
    qd = qd_ref[0]
    grp = DSA_HEADS // DSA_KV_HEADS
    dh = DSA_HEAD_DIM
    outs = []
    for g in range(DSA_KV_HEADS):
        qg = jnp.concatenate([qd[:, (g * grp + i) * dh:(g * grp + i + 1) * dh] for i in range(grp)], axis=0)

        def att_chunk(c, carry, g=g, qg=qg):
            m, l, acc = carry
            off = pl.multiple_of(c * ck, ck)
            k = kd_ref[0, pl.ds(off, ck), :][:, g * dh:(g + 1) * dh]
            v = vd_ref[0, pl.ds(off, ck), :][:, g * dh:(g + 1) * dh]
            s = lax.dot_general(qg, k, _NT, preferred_element_type=F32)
            pk = jnp.concatenate([posk_ref[0, c * sub + t] for t in range(sub)], axis=1)
            dist = jnp.abs(posq - pk).astype(F32)
            msk = jnp.concatenate([mask_ref[c * sub + t] for t in range(sub)], axis=1) > 0.5
            parts = []
            for i in range(grp):
                slope = 2.0 ** (-8.0 * (g * grp + i + 1) / DSA_HEADS)
                parts.append(jnp.where(msk, s[i * qb:(i + 1) * qb] - slope * dist, NEG_INF))
            s = jnp.concatenate(parts, axis=0)
            m_new = jnp.maximum(m, jnp.max(s, axis=-1, keepdims=True))
            alpha = jnp.exp(m - m_new)
            p = jnp.exp(s - m_new)
            l = alpha * l + jnp.sum(p, axis=-1, keepdims=True)
            acc = alpha * acc + jnp.dot(p.astype(BF16), v, preferred_element_type=F32)
            return m_new, l, acc

        init = (jnp.full((grp * qb, 1), NEG_INF, F32), jnp.zeros((grp * qb, 1), F32),
                jnp.zeros((grp * qb, dh), F32))
        _, l, acc = lax.fori_loop(0, nch, att_chunk, init)
        o = acc / l
        outs.extend(o[i * qb:(i + 1) * qb] for i in range(grp))
    y_ref[0] = jnp.concatenate(outs, axis=1).astype(BF16)


def _dsa(qi, kiw, qd, kd, vd, posq, posk):
    b, s, _ = qd.shape
    qb = QB_DSA
    blk = lambda w: pl.BlockSpec((1, qb, w), lambda bi, j: (bi, j, 0))
    seq = lambda w: pl.BlockSpec((1, s, w), lambda bi, j: (bi, 0, 0))
    return pl.pallas_call(
        _dsa_kernel,
        out_shape=jax.ShapeDtypeStruct((b, s, qd.shape[2]), BF16),
        grid=(b, s // qb),
        in_specs=[blk(qi.shape[2]), seq(kiw.shape[2]), blk(qd.shape[2]), seq(kd.shape[2]), seq(vd.shape[2]),
                  blk(1),
                  pl.BlockSpec((1, s // qb, 1, qb), lambda bi, j: (bi, 0, 0, 0))],
        out_specs=blk(qd.shape[2]),
        scratch_shapes=[pltpu.VMEM((s, qb), F32), pltpu.VMEM((s // qb, qb, qb), F32)],
        compiler_params=_cparams(("parallel", "arbitrary")),
        name="dsa",
    )(qi, kiw, qd, kd, vd, posq, posk)


_META_ROWS = 16


def _post_kernel(ym_ref, yd_ref, x_ref, mod_ref, woa_ref, wob_ref, gpost_ref, gpre_ref, wr_ref, br_ref,
                 x1_ref, h2_ref, meta_ref, gates_ref, cnt_ref):
    i = pl.program_id(0)
    d = x_ref.shape[1]
    tm = x_ref.shape[0]

    @pl.when(i == 0)
    def _():
        cnt_ref[...] = jnp.zeros_like(cnt_ref)

    mix = (jnp.dot(ym_ref[...], woa_ref[...], preferred_element_type=F32)
           + jnp.dot(yd_ref[...], wob_ref[...], preferred_element_type=F32))
    gt1 = mod_ref[0, :, 2 * d:3 * d]
    sh2 = mod_ref[0, :, 3 * d:4 * d]
    sc2 = mod_ref[0, :, 4 * d:5 * d]
    x1 = x_ref[...] + gt1 * _rms(mix, gpost_ref[...])
    x1_ref[...] = x1
    h2 = _rms(x1, gpre_ref[...]) * (1.0 + sc2) + sh2
    h2_ref[...] = h2

    logits = jnp.dot(h2, wr_ref[...], preferred_element_type=F32,
                     precision=lax.Precision.HIGHEST) + br_ref[...]
    ne = logits.shape[1]
    lane = lax.broadcasted_iota(I32, (tm, ne), 1).astype(F32)
    work = logits
    ids, vals = [], []
    for _ in range(TOP_K):
        mx = jnp.max(work, axis=-1, keepdims=True)
        idx = jnp.min(jnp.where(work == mx, lane, float(ne)), axis=-1, keepdims=True)
        ids.append(idx)
        vals.append(mx)
        work = jnp.where(lane == idx, -jnp.inf, work)
    es = [jnp.exp(v - vals[0]) for v in vals]
    den = es[0] + es[1] + es[2] + es[3]
    gates = [e / den for e in es]

    member = jnp.zeros((tm, ne), F32)
    for idx in ids:
        member = member + (lane == idx).astype(F32)
    r_i = lax.broadcasted_iota(I32, (tm, tm), 0)
    c_i = lax.broadcasted_iota(I32, (tm, tm), 1)
    tri = (c_i < r_i).astype(BF16)
    before = jnp.dot(tri, member.astype(BF16), preferred_element_type=F32) + cnt_ref[...]
    cnt_ref[...] = cnt_ref[...] + jnp.sum(member, axis=0, keepdims=True)
    ranks = [jnp.sum(jnp.where(lane == idx, before, 0.0), axis=-1, keepdims=True) for idx in ids]

    lane_m = lax.broadcasted_iota(I32, (tm, LANES), 1)
    rec = jnp.zeros((tm, LANES), F32)
    for k, col in enumerate(ids + gates + ranks):
        rec = jnp.where(lane_m == k, col, rec)
    meta_ref[...] = rec.T[0:_META_ROWS, :]
    lane_g = lax.broadcasted_iota(I32, (tm, TOP_K), 1)
    gm = jnp.zeros((tm, TOP_K), F32)
    for k, col in enumerate(gates):
        gm = jnp.where(lane_g == k, col, gm)
    gates_ref[...] = gm


def _post(ym, yd, x2, mod3, woa, wob, g_post, g_pre, w_router, b_router, tiles_per_seq):
    n, d = x2.shape
    tm = TM_POST
    ne = w_router.shape[1]
    tok = lambda w: pl.BlockSpec((tm, w), lambda i: (i, 0))
    full = lambda a: pl.BlockSpec(a.shape, lambda i: (0,) * a.ndim)
    return pl.pallas_call(
        _post_kernel,
        out_shape=[jax.ShapeDtypeStruct((n, d), F32), jax.ShapeDtypeStruct((n, d), F32),
                   jax.ShapeDtypeStruct((_META_ROWS, n), F32), jax.ShapeDtypeStruct((n, TOP_K), F32)],
        grid=(n // tm,),
        in_specs=[tok(ym.shape[1]), tok(yd.shape[1]), tok(d),
                  pl.BlockSpec((1, 1, mod3.shape[2]), lambda i: (i // tiles_per_seq, 0, 0)),
                  full(woa), full(wob), full(g_post), full(g_pre), full(w_router), full(b_router)],
        out_specs=[tok(d), tok(d), pl.BlockSpec((_META_ROWS, tm), lambda i: (0, i)), tok(TOP_K)],
        scratch_shapes=[pltpu.VMEM((1, ne), F32)],
        compiler_params=_cparams(("arbitrary",)),
        name="post",
    )(ym, yd, x2, mod3, woa, wob, g_post, g_pre, w_router, b_router)


def _route_kernel(meta_ref, dest_ref, blk_ref):
    ne = N_EXPERTS
    nbl = blk_ref.shape[1]
    e_col = lax.broadcasted_iota(I32, (ne, 1), 0).astype(F32)
    onehots = [(meta_ref[k:k + 1, :] == e_col) for k in range(TOP_K)]
    counts = jnp.zeros((ne, 1), F32)
    for oh in onehots:
        counts = counts + jnp.sum(oh.astype(F32), axis=1, keepdims=True)
    padded = jnp.floor((counts + float(BM_FFN - 1)) / BM_FFN) * BM_FFN
    r_i = lax.broadcasted_iota(I32, (ne, ne), 0)
    c_i = lax.broadcasted_iota(I32, (ne, ne), 1)
    tri = (c_i < r_i).astype(F32)
    padded_f = jnp.broadcast_to(padded, (ne, LANES))
    start = jnp.dot(tri, padded_f, preferred_element_type=F32, precision=lax.Precision.HIGHEST)[:, 0:1]
    for k in range(TOP_K):
        add = jnp.sum(jnp.where(onehots[k], start, 0.0), axis=0, keepdims=True)
        dest_ref[k:k + 1, :] = (meta_ref[2 * TOP_K + k:2 * TOP_K + k + 1, :] + add).astype(I32)
    end = start + padded
    row0 = (lax.broadcasted_iota(I32, (1, nbl), 1) * BM_FFN).astype(F32)
    blk_e = jnp.minimum(jnp.sum((end <= row0).astype(F32), axis=0, keepdims=True), float(ne - 1))
    own = blk_e == e_col
    real_end = jnp.sum(jnp.where(own, start + counts, 0.0), axis=0, keepdims=True)
    nvalid = jnp.clip(real_end - row0, 0.0, float(BM_FFN))
    nact = jnp.sum(padded, axis=0, keepdims=True) / BM_FFN
    rows = lax.broadcasted_iota(I32, blk_ref.shape, 0)
    out = jnp.where(rows == 0, blk_e, jnp.where(rows == 1, nvalid, jnp.broadcast_to(nact, blk_ref.shape)))
    blk_ref[...] = out.astype(I32)


def _route(meta, n_blocks):
    n = meta.shape[1]
    nbl = (n_blocks + LANES - 1) // LANES * LANES
    return pl.pallas_call(
        _route_kernel,
        out_shape=[jax.ShapeDtypeStruct((TOP_K, n), I32), jax.ShapeDtypeStruct((8, nbl), I32)],
        compiler_params=pltpu.CompilerParams(vmem_limit_bytes=VMEM_LIMIT),
        name="route",
    )(meta)


def _dispatch_kernel(dest_ref, h_ref, zero_ref, xs_ref, sem):
    del zero_ref
    i = pl.program_id(0)
    tm = h_ref.shape[0]
    n = pl.num_programs(0) * tm

    def row_copy(t, k):
        dst = dest_ref[k * n + i * tm + t]
        return pltpu.make_async_copy(h_ref.at[pl.ds(t, 1), :], xs_ref.at[pl.ds(dst, 1), :], sem)

    def start(t, c):
        for k in range(TOP_K):
            row_copy(t, k).start()
        return c

    def wait(t, c):
        for k in range(TOP_K):
            row_copy(t, k).wait()
        return c

    lax.fori_loop(0, tm, start, 0)
    lax.fori_loop(0, tm, wait, 0)


def _dispatch(dest_flat, h2, n_slots):
    n, d = h2.shape
    tm = TM_ROWS
    return pl.pallas_call(
        _dispatch_kernel,
        out_shape=jax.ShapeDtypeStruct((n_slots, d), h2.dtype),
        grid_spec=pltpu.PrefetchScalarGridSpec(
            num_scalar_prefetch=1,
            grid=(n // tm,),
            in_specs=[pl.BlockSpec((tm, d), lambda i, dest: (i, 0)),
                      pl.BlockSpec(memory_space=pl.ANY)],
            out_specs=pl.BlockSpec(memory_space=pl.ANY),
            scratch_shapes=[pltpu.SemaphoreType.DMA],
        ),
        input_output_aliases={2: 0},
        compiler_params=_cparams(("arbitrary",)),
        name="dispatch",
    )(dest_flat, h2, jnp.zeros((n_slots, d), h2.dtype))


def _ffn_kernel(be_ref, nv_ref, na_ref, xs_ref, w1g_ref, w1l_ref, w2_ref, b1g_ref, b1l_ref, b2_ref, y_ref):
    b = pl.program_id(0)

    @pl.when(b < na_ref[0])
    def _():
        bm = xs_ref.shape[0]
        rows = lax.broadcasted_iota(I32, (bm, 1), 0)
        x = jnp.where(rows < nv_ref[b], xs_ref[...], 0.0).astype(BF16)
        glu = jnp.dot(x, w1g_ref[0], preferred_element_type=F32) + b1g_ref[0]
        lin = jnp.dot(x, w1l_ref[0], preferred_element_type=F32) + b1l_ref[0]
        glu = jnp.minimum(glu, SWIGLU_LIMIT)
        lin = jnp.clip(lin, -SWIGLU_LIMIT, SWIGLU_LIMIT)
        act = glu * jax.nn.sigmoid(SWIGLU_ALPHA * glu) * (lin + 1.0)
        y_ref[...] = jnp.dot(act.astype(BF16), w2_ref[0], preferred_element_type=F32) + b2_ref[0]

    @pl.when(b >= na_ref[0])
    def _():
        y_ref[...] = jnp.zeros_like(y_ref)


def _ffn(blk_e, blk_nv, n_act, xs, w1g, w1l, w2, b1g, b1l, b2):
    n_slots, d = xs.shape
    bm = BM_FFN
    f = w1g.shape[2]
    blk = lambda b, be, nv, na: (jnp.minimum(b, na[0] - 1), 0)
    exp3 = lambda b, be, nv, na: (be[jnp.minimum(b, na[0] - 1)], 0, 0)
    return pl.pallas_call(
        _ffn_kernel,
        out_shape=jax.ShapeDtypeStruct((n_slots, d), F32),
        grid_spec=pltpu.PrefetchScalarGridSpec(
            num_scalar_prefetch=3,
            grid=(n_slots // bm,),
            in_specs=[pl.BlockSpec((bm, d), blk),
                      pl.BlockSpec((1, d, f), exp3), pl.BlockSpec((1, d, f), exp3),
                      pl.BlockSpec((1, f, d), exp3),
                      pl.BlockSpec((1, 1, f), exp3), pl.BlockSpec((1, 1, f), exp3),
                      pl.BlockSpec((1, 1, d), exp3)],
            out_specs=pl.BlockSpec((bm, d), lambda b, be, nv, na: (b, 0)),
        ),
        compiler_params=_cparams(("arbitrary",)),
        name="ffn",
    )(blk_e, blk_nv, n_act, xs, w1g, w1l, w2, b1g, b1l, b2)


def _combine_kernel(dest_ref, y_ref, gates_ref, x1_ref, mod_ref, gpost_ref, o_ref, buf_ref, sem):
    i = pl.program_id(0)
    tm, d = x1_ref.shape
    n = pl.num_programs(0) * tm

    def row_copy(t, k):
        src = dest_ref[k * n + i * tm + t]
        return pltpu.make_async_copy(y_ref.at[pl.ds(src, 1), :], buf_ref.at[k, pl.ds(t, 1), :], sem)

    def start(t, c):
        for k in range(TOP_K):
            row_copy(t, k).start()
        return c

    def wait(t, c):
        for k in range(TOP_K):
            row_copy(t, k).wait()
        return c

    lax.fori_loop(0, tm, start, 0)
    lax.fori_loop(0, tm, wait, 0)
    g = gates_ref[...]
    ffn = g[:, 0:1] * buf_ref[0]
    for k in range(1, TOP_K):
        ffn = ffn + g[:, k:k + 1] * buf_ref[k]
    gt2 = mod_ref[0, :, 5 * d:6 * d]
    o_ref[...] = x1_ref[...] + gt2 * _rms(ffn, gpost_ref[...])


def _combine(dest_flat, y, gates, x1, mod3, g_post, tiles_per_seq):
    n, d = x1.shape
    tm = TM_ROWS
    tok = lambda w: pl.BlockSpec((tm, w), lambda i, dest: (i, 0))
    return pl.pallas_call(
        _combine_kernel,
        out_shape=jax.ShapeDtypeStruct((n, d), F32),
        grid_spec=pltpu.PrefetchScalarGridSpec(
            num_scalar_prefetch=1,
            grid=(n // tm,),
            in_specs=[pl.BlockSpec(memory_space=pl.ANY), tok(TOP_K), tok(d),
                      pl.BlockSpec((1, 1, mod3.shape[2]), lambda i, dest: (i // tiles_per_seq, 0, 0)),
                      pl.BlockSpec(g_post.shape, lambda i, dest: (0, 0))],
            out_specs=tok(d),
            scratch_shapes=[pltpu.VMEM((TOP_K, tm, d), F32), pltpu.SemaphoreType.DMA],
        ),
        compiler_params=_cparams(("arbitrary",)),
        name="combine",
    )(dest_flat, y, gates, x1, mod3, g_post)


def _layer(x, mod, positions, g_pre_mix, g_post_mix, g_pre_ffn, g_post_ffn, w_in, g_q_a, g_kv_a,
           w_q_b, w_kv_b, w_o, w_router, b_router, w_mlp1, b_mlp1, w_mlp2, b_mlp2):
    b, s, d = x.shape
    n = b * s
    mod3 = mod.reshape(b, 1, mod.shape[1])
    row = lambda g: g.reshape(1, -1)

    win, wq, wkv = _relayout_in_weights(w_in, w_q_b, w_kv_b)
    qm, km, vm, qd, kd, vd, qi, kiw = _inproj(
        x, mod3, positions.reshape(b, s, 1), row(g_pre_mix), win, row(g_q_a), row(g_kv_a), wq, wkv,
        _rope_rows())
    y_mla = _mla(qm, km, vm)
    y_dsa = _dsa(qi, kiw, qd, kd, vd, positions.reshape(b, s, 1),
                 positions.reshape(b, s // QB_DSA, 1, QB_DSA))

    wo = w_o.astype(BF16)
    split = MLA_HEADS * MLA_V
    x1, h2, meta, gates = _post(
        y_mla.reshape(n, -1), y_dsa.reshape(n, -1), x.reshape(n, d), mod3, wo[:split], wo[split:],
        row(g_post_mix), row(g_pre_ffn), w_router, row(b_router), s // TM_POST)

    n_slots = n * TOP_K + N_EXPERTS * BM_FFN
    dest, blk = _route(meta, n_slots // BM_FFN)
    dest_flat = dest.reshape(-1)
    xs = _dispatch(dest_flat, h2, n_slots)
    f = w_mlp2.shape[1]
    w1g = w_mlp1[:, :, 0::2].astype(BF16)
    w1l = w_mlp1[:, :, 1::2].astype(BF16)
    y = _ffn(blk[0], blk[1], blk[2, 0:1], xs, w1g, w1l, w_mlp2.astype(BF16),
             b_mlp1[:, 0::2].reshape(-1, 1, f), b_mlp1[:, 1::2].reshape(-1, 1, f),
             b_mlp2.reshape(-1, 1, d))
    out = _combine(dest_flat, y, gates, x1, mod3, row(g_post_ffn), s // TM_ROWS)
    return out.reshape(b, s, d)


def kernel(x, c, positions, w_ada, b_ada, g_pre_mix, g_post_mix, g_pre_ffn, g_post_ffn, w_in, g_q_a, g_kv_a, w_q_b, w_kv_b, w_o, w_router, b_router, w_mlp1, b_mlp1, w_mlp2, b_mlp2):
    for l in range(w_ada.shape[0]):
        mod = _ada(c, w_ada[l], b_ada[l])
        x = _layer(x, mod, positions, g_pre_mix[l], g_post_mix[l], g_pre_ffn[l], g_post_ffn[l],
                   w_in[l], g_q_a[l], g_kv_a[l], w_q_b[l], w_kv_b[l], w_o[l], w_router[l], b_router[l],
                   w_mlp1[l], b_mlp1[l], w_mlp2[l], b_mlp2[l])
    return x
```

```python
import functools

import jax
import jax.numpy as jnp
from jax import lax
from jax.experimental import pallas as pl
from jax.experimental.pallas import tpu as pltpu

F32 = jnp.float32
BF16 = jnp.bfloat16
I32 = jnp.int32

MLA_HEADS = 8
MLA_Q_RANK = 256
MLA_KV_RANK = 128
MLA_NOPE = 64
MLA_ROPE = 32
MLA_V = 64
ROPE_THETA = 10000.0
DSA_HEADS = 8
DSA_KV_HEADS = 2
DSA_HEAD_DIM = 64
IDX_HEADS = 8
IDX_DIM = 32
IDX_TOPK_MAX = 256
N_EXPERTS = 32
TOP_K = 4
SWIGLU_ALPHA = 1.702
SWIGLU_LIMIT = 7.0
NORM_EPS = 1e-6
NEG_INF = -1e30
LOG2E = 1.4426950408889634

LANES = 128
HEAD_PAD = 128
VMEM_LIMIT = 52 * 1024 * 1024

TM_PROJ = 512
KV_CHUNK = 256
TQ_MLA = 512
QB_DSA = 128
TM_POST = 512
BM_FFN = 256
TM_ROWS = 256

_NT = (((1,), (1,)), ((), ()))


def _rms(x, g):
    ms = jnp.mean(x * x, axis=-1, keepdims=True)
    return x * lax.rsqrt(ms + NORM_EPS) * g


def _cparams(sem, vmem=VMEM_LIMIT):
    return pltpu.CompilerParams(dimension_semantics=sem, vmem_limit_bytes=vmem)


def _ada_kernel(c_ref, w_ref, b_ref, o_ref):
    c = c_ref[...]
    cond = c * jax.nn.sigmoid(c)
    o_ref[...] = jnp.dot(cond.astype(BF16), w_ref[...].astype(BF16),
                         preferred_element_type=F32) + b_ref[...]


def _ada(c, w_ada, b_ada):
    b, d = c.shape
    n = w_ada.shape[1]
    tn = 1024
    return pl.pallas_call(
        _ada_kernel,
        out_shape=jax.ShapeDtypeStruct((b, n), F32),
        grid=(n // tn,),
        in_specs=[pl.BlockSpec((b, d), lambda i: (0, 0)),
                  pl.BlockSpec((d, tn), lambda i: (0, i)),
                  pl.BlockSpec((1, tn), lambda i: (0, i))],
        out_specs=pl.BlockSpec((b, tn), lambda i: (0, i)),
        compiler_params=_cparams(("arbitrary",)),
        name="ada",
    )(c, w_ada, b_ada.reshape(1, n))


_O_CQ = 0
_O_CKV = _O_CQ + MLA_Q_RANK
_O_KRA = _O_CKV + MLA_KV_RANK
_O_KRB = _O_KRA + HEAD_PAD
_O_QD = _O_KRB + HEAD_PAD
_O_KD = _O_QD + DSA_HEADS * DSA_HEAD_DIM
_O_VD = _O_KD + DSA_KV_HEADS * DSA_HEAD_DIM
_O_QI = _O_VD + DSA_KV_HEADS * DSA_HEAD_DIM
_O_KIW = _O_QI + IDX_HEADS * IDX_DIM
_W_IN = _O_KIW + LANES


def _inproj_kernel(x_ref, mod_ref, pos_ref, gpre_ref, win_ref, gq_ref, gkv_ref, wq_ref, wkv_ref,
                   rope_ref, qm_ref, km_ref, vm_ref, qd_ref, kd_ref, vd_ref, qi_ref, kiw_ref):
    d = x_ref.shape[2]
    x = x_ref[0]
    sh1 = mod_ref[0, :, 0:d]
    sc1 = mod_ref[0, :, d:2 * d]
    h = _rms(x, gpre_ref[...]) * (1.0 + sc1) + sh1
    proj = jnp.dot(h.astype(BF16), win_ref[...], preferred_element_type=F32)

    ang = pos_ref[0].astype(F32) * rope_ref[0:1, :]
    cos_t = jnp.cos(ang) * rope_ref[1:2, :] + rope_ref[3:4, :]
    sin_t = jnp.sin(ang) * rope_ref[2:3, :]

    hw = MLA_HEADS * HEAD_PAD
    nq = _rms(proj[:, _O_CQ:_O_CQ + MLA_Q_RANK], gq_ref[...])
    qab = jnp.dot(nq.astype(BF16), wq_ref[...], preferred_element_type=F32)
    scale = (MLA_NOPE + MLA_ROPE) ** -0.5 * LOG2E
    nkv = _rms(proj[:, _O_CKV:_O_CKV + MLA_KV_RANK], gkv_ref[...])
    kv = jnp.dot(nkv.astype(BF16), wkv_ref[...], preferred_element_type=F32)
    kr = proj[:, _O_KRA:_O_KRA + HEAD_PAD] * cos_t + proj[:, _O_KRB:_O_KRB + HEAD_PAD] * sin_t
    for hd in range(MLA_HEADS):
        sl = slice(hd * HEAD_PAD, (hd + 1) * HEAD_PAD)
        slb = slice(hw + hd * HEAD_PAD, hw + (hd + 1) * HEAD_PAD)
        qm_ref[0, :, sl] = ((qab[:, sl] * cos_t + qab[:, slb] * sin_t) * scale).astype(BF16)
        km_ref[0, :, sl] = (kv[:, sl] + kr).astype(BF16)

    qd_ref[0] = (proj[:, _O_QD:_O_KD] * (DSA_HEAD_DIM ** -0.5 * LOG2E)).astype(BF16)
    kd_ref[0] = proj[:, _O_KD:_O_VD].astype(BF16)
    qi_ref[0] = (proj[:, _O_QI:_O_KIW] * (IDX_DIM ** -0.5)).astype(BF16)
    kiw_ref[0] = proj[:, _O_KIW:_W_IN]
    ck = vm_ref.shape[3]
    for t in range(x_ref.shape[1] // ck):
        rows = slice(t * ck, (t + 1) * ck)
        vm_ref[0, t] = kv[rows, hw:hw + MLA_HEADS * MLA_V].T.astype(BF16)
        vd_ref[0, t] = proj[rows, _O_VD:_O_QI].T.astype(BF16)


def _relayout_in_weights(w_in, w_q_b, w_kv_b):
    d = w_in.shape[0]
    half = MLA_ROPE // 2
    o = 0
    segs = {}
    for name, width in (("cq", MLA_Q_RANK), ("ckv", MLA_KV_RANK), ("kr", MLA_ROPE),
                        ("qd", DSA_HEADS * DSA_HEAD_DIM), ("kd", DSA_KV_HEADS * DSA_HEAD_DIM),
                        ("vd", DSA_KV_HEADS * DSA_HEAD_DIM), ("qi", IDX_HEADS * IDX_DIM),
                        ("ki", IDX_DIM), ("wi", IDX_HEADS)):
        segs[name] = w_in[:, o:o + width]
        o += width
    z = lambda n: jnp.zeros((d, n), w_in.dtype)
    x1, x2 = segs["kr"][:, :half], segs["kr"][:, half:]
    tail = HEAD_PAD - MLA_NOPE - MLA_ROPE
    kra = jnp.concatenate([z(MLA_NOPE), x1, x2, z(tail)], axis=1)
    krb = jnp.concatenate([z(MLA_NOPE), x2, x1, z(tail)], axis=1)
    win = jnp.concatenate([segs["cq"], segs["ckv"], kra, krb, segs["qd"], segs["kd"], segs["vd"],
                           segs["qi"], segs["ki"], segs["wi"], z(LANES - IDX_DIM - IDX_HEADS)], axis=1)
    r = w_q_b.shape[0]
    wq = w_q_b.reshape(r, MLA_HEADS, MLA_NOPE + MLA_ROPE)
    zq = lambda n: jnp.zeros((r, MLA_HEADS, n), w_q_b.dtype)
    wqa = jnp.concatenate([wq, zq(tail)], axis=2).reshape(r, MLA_HEADS * HEAD_PAD)
    wqb = jnp.concatenate([zq(MLA_NOPE), wq[:, :, MLA_NOPE + half:], wq[:, :, MLA_NOPE:MLA_NOPE + half],
                           zq(tail)], axis=2).reshape(r, MLA_HEADS * HEAD_PAD)
    rk = w_kv_b.shape[0]
    wkv = w_kv_b.reshape(rk, MLA_HEADS, MLA_NOPE + MLA_V)
    wkn = jnp.concatenate([wkv[:, :, :MLA_NOPE], jnp.zeros((rk, MLA_HEADS, HEAD_PAD - MLA_NOPE), w_kv_b.dtype)],
                          axis=2).reshape(rk, MLA_HEADS * HEAD_PAD)
    wv = wkv[:, :, MLA_NOPE:].reshape(rk, MLA_HEADS * MLA_V)
    return (win.astype(BF16), jnp.concatenate([wqa, wqb], axis=1).astype(BF16),
            jnp.concatenate([wkn, wv], axis=1).astype(BF16))


def _rope_rows():
    half = MLA_ROPE // 2
    lane = jnp.arange(LANES)
    freqs = ROPE_THETA ** (-jnp.arange(half, dtype=F32) / half)
    in_x1 = (lane >= MLA_NOPE) & (lane < MLA_NOPE + half)
    in_x2 = (lane >= MLA_NOPE + half) & (lane < MLA_NOPE + MLA_ROPE)
    fr = jnp.where(in_x1 | in_x2, freqs[(lane - MLA_NOPE) % half], 0.0)
    cosm = (in_x1 | in_x2).astype(F32)
    sinm = jnp.where(in_x1, -1.0, jnp.where(in_x2, 1.0, 0.0))
    nopem = (lane < MLA_NOPE).astype(F32)
    rows = jnp.stack([fr, cosm, sinm, nopem], axis=0).astype(F32)
    return jnp.concatenate([rows, jnp.zeros((4, LANES), F32)], axis=0)


def _inproj(x, mod3, pos3, g_pre, win, g_q, g_kv, wq, wkv, rope_rows):
    b, s, d = x.shape
    tm = TM_PROJ
    ck = KV_CHUNK
    tok = lambda w: pl.BlockSpec((1, tm, w), lambda bi, i: (bi, i, 0))
    full = lambda a: pl.BlockSpec(a.shape, lambda bi, i: (0,) * a.ndim)
    tr = lambda w: pl.BlockSpec((1, tm // ck, w, ck), lambda bi, i: (bi, i, 0, 0))
    tok_out = lambda w, dt: (jax.ShapeDtypeStruct((b, s, w), dt), tok(w))
    tr_out = lambda w: (jax.ShapeDtypeStruct((b, s // ck, w, ck), BF16), tr(w))
    outs = [tok_out(MLA_HEADS * HEAD_PAD, BF16), tok_out(MLA_HEADS * HEAD_PAD, BF16),
            tr_out(MLA_HEADS * MLA_V), tok_out(DSA_HEADS * DSA_HEAD_DIM, BF16),
            tok_out(DSA_KV_HEADS * DSA_HEAD_DIM, BF16), tr_out(DSA_KV_HEADS * DSA_HEAD_DIM),
            tok_out(IDX_HEADS * IDX_DIM, BF16), tok_out(LANES, F32)]
    return pl.pallas_call(
        _inproj_kernel,
        out_shape=[o[0] for o in outs],
        grid=(b, s // tm),
        in_specs=[tok(d),
                  pl.BlockSpec((1, 1, mod3.shape[2]), lambda bi, i: (bi, 0, 0)),
                  tok(1), full(g_pre), full(win), full(g_q), full(g_kv), full(wq), full(wkv),
                  full(rope_rows)],
        out_specs=[o[1] for o in outs],
        compiler_params=_cparams(("parallel", "arbitrary")),
        name="inproj",
    )(x, mod3, pos3, g_pre, win, g_q, g_kv, wq, wkv, rope_rows)


def _mla_kernel(q_ref, k_ref, vt_ref, o_ref, m_ref, l_ref, acc_ref):
    j = pl.program_id(1)
    tq = q_ref.shape[1]
    tk = k_ref.shape[1] // vt_ref.shape[1]
    m_ref[...] = jnp.full_like(m_ref, NEG_INF)
    l_ref[...] = jnp.zeros_like(l_ref)
    acc_ref[...] = jnp.zeros_like(acc_ref)
    per_q = tq // tk
    krow = lax.broadcasted_iota(I32, (tk, 1), 0)
    qcol = j * tq + lax.broadcasted_iota(I32, (1, tq), 1)

    def chunk(c, diagonal):
        off = pl.multiple_of(c * tk, tk)
        m_all = m_ref[...]
        l_all = l_ref[...]
        accs = [acc_ref[hd] for hd in range(MLA_HEADS)]
        ms, ls = [], []

        def scores(hd):
            lanes = slice(hd * HEAD_PAD, (hd + 1) * HEAD_PAD)
            return lax.dot_general(k_ref[0, pl.ds(off, tk), lanes], q_ref[0, :, lanes], _NT,
                                   preferred_element_type=F32)

        s_next = scores(0)
        pend = None
        for hd in range(MLA_HEADS):
            s = s_next
            if hd + 1 < MLA_HEADS:
                s_next = scores(hd + 1)
            if diagonal:
                s = jnp.where(off + krow <= qcol, s, NEG_INF)
            m_old = m_all[hd:hd + 1, :]
            m_new = jnp.maximum(m_old, jnp.max(s, axis=0, keepdims=True))
            alpha = jnp.exp2(m_old - m_new)
            p = jnp.exp2(s - m_new)
            ls.append(alpha * l_all[hd:hd + 1, :] + jnp.sum(p, axis=0, keepdims=True))
            ms.append(m_new)
            pv = jnp.dot(vt_ref[0, c, hd * MLA_V:(hd + 1) * MLA_V, :], p.astype(BF16),
                         preferred_element_type=F32)
            if pend is not None:
                accs[pend[0]] = pend[1] * accs[pend[0]] + pend[2]
            pend = (hd, alpha, pv)
        accs[pend[0]] = pend[1] * accs[pend[0]] + pend[2]
        m_ref[...] = jnp.concatenate(ms, axis=0)
        l_ref[...] = jnp.concatenate(ls, axis=0)
        for hd in range(MLA_HEADS):
            acc_ref[hd] = accs[hd]

    def body(c, carry):
        chunk(c, False)
        return carry

    lax.fori_loop(0, j * per_q, body, 0)
    for t in range(per_q):
        chunk(j * per_q + t, True)
    o_t = jnp.concatenate([acc_ref[hd] / l_ref[hd:hd + 1, :] for hd in range(MLA_HEADS)], axis=0)
    o_ref[0] = o_t.T.astype(BF16)


def _mla(qm, km, vmt):
    b, s, hw = qm.shape
    tq = TQ_MLA
    _, nck, vw, ck = vmt.shape
    return pl.pallas_call(
        _mla_kernel,
        out_shape=jax.ShapeDtypeStruct((b, s, vw), BF16),
        grid=(b, s // tq),
        in_specs=[pl.BlockSpec((1, tq, hw), lambda bi, j: (bi, j, 0)),
                  pl.BlockSpec((1, s, hw), lambda bi, j: (bi, 0, 0)),
                  pl.BlockSpec((1, nck, vw, ck), lambda bi, j: (bi, 0, 0, 0))],
        out_specs=pl.BlockSpec((1, tq, vw), lambda bi, j: (bi, j, 0)),
        scratch_shapes=[pltpu.VMEM((MLA_HEADS, tq), F32), pltpu.VMEM((MLA_HEADS, tq), F32),
                        pltpu.VMEM((MLA_HEADS, MLA_V, tq), F32)],
        compiler_params=_cparams(("parallel", "arbitrary")),
        name="mla",
    )(qm, km, vmt)


_KEY_NEG_INF = -2139095041
_KEY_POS_INF = 2139095040
_I32_MAX = 2147483647


def _key_to_f32(k):
    bits = k ^ ((k >> 31) & _I32_MAX)
    return lax.bitcast_convert_type(bits, F32)


def _dsa_kernel(qi_ref, kiw_ref, qd_ref, kd_ref, vdt_ref, posk_ref, posq_ref, y_ref, sc_ref, acc_ref):
    j = pl.program_id(1)
    qb, ck = QB_DSA, KV_CHUNK
    nch = (j * qb + qb + ck - 1) // ck
    n_sel = IDX_TOPK_MAX

    kiw_q = kiw_ref[0, pl.ds(pl.multiple_of(j * qb, qb), qb), :]
    w_t = kiw_q.T[IDX_DIM:IDX_DIM + IDX_HEADS, :] * (IDX_HEADS ** -0.5)
    qi = qi_ref[0]
    qi_stack = jnp.concatenate([qi[:, hd * IDX_DIM:(hd + 1) * IDX_DIM] for hd in range(IDX_HEADS)], axis=0)
    q_idx = j * qb + lax.broadcasted_iota(I32, (1, qb), 1)
    k_iota = lax.broadcasted_iota(I32, (ck, 1), 0)

    def idx_chunk(c, carry):
        off = pl.multiple_of(c * ck, ck)
        ki = kiw_ref[0, pl.ds(off, ck), :][:, 0:IDX_DIM].astype(BF16)
        r = lax.dot_general(ki, qi_stack, _NT, preferred_element_type=F32)
        acc = jnp.zeros((ck, qb), F32)
        for hd in range(IDX_HEADS):
            acc = acc + w_t[hd:hd + 1, :] * jnp.maximum(r[:, hd * qb:(hd + 1) * qb], 0.0)
        sc_ref[pl.ds(off, ck), :] = jnp.where(off + k_iota <= q_idx, acc, NEG_INF)
        return carry

    lax.fori_loop(0, nch, idx_chunk, 0)

    n_part = 4

    def count(pred_fn):
        def body(c, parts):
            off = pl.multiple_of(c * ck, ck)
            hit = pred_fn(sc_ref[pl.ds(off, ck), :], off).astype(F32)
            rows = ck // n_part
            return tuple(p + jnp.sum(hit[i * rows:(i + 1) * rows].reshape(rows // 8, 8, qb), axis=0)
                         for i, p in enumerate(parts))
        parts = lax.fori_loop(0, nch, body, (jnp.zeros((8, qb), F32),) * n_part)
        return jnp.sum((parts[0] + parts[1]) + (parts[2] + parts[3]), axis=0, keepdims=True)

    def select(_):
        def bisect(_, st):
            lo, hi, c_lo = st
            mid = (lo & hi) + ((lo ^ hi) >> 1)
            t = _key_to_f32(mid)
            cnt = count(lambda x, off: x >= t)
            ok = cnt >= n_sel
            return jnp.where(ok, mid, lo), jnp.where(ok, hi, mid), jnp.where(ok, cnt, c_lo)

        per_round = 4

        def round_(st):
            it, lo, hi, c_lo = st
            lo, hi, c_lo = lax.fori_loop(0, per_round, bisect, (lo, hi, c_lo))
            return it + per_round, lo, hi, c_lo

        def unsettled(st):
            it, _, _, c_lo = st
            return (it < 32) & (jnp.max(jnp.abs(c_lo - n_sel)) > 0.5)

        st0 = (jnp.int32(0), jnp.full((1, qb), _KEY_NEG_INF, I32), jnp.full((1, qb), _KEY_POS_INF, I32),
               jnp.broadcast_to((nch * ck).astype(F32), (1, qb)))
        _, lo, _, c_lo = lax.while_loop(unsettled, round_, st0)
        t_lo = _key_to_f32(lo)

        def tie_search(_):
            t_hi = _key_to_f32(lo + 1)
            need = n_sel - count(lambda x, off: x >= t_hi)

            def step(_, lohi):
                mlo, mhi = lohi
                mid = (mlo + mhi) >> 1
                cnt = count(lambda x, off: (x >= t_lo) & jnp.logical_not(x >= t_hi)
                            & (off + k_iota <= mid))
                ok = cnt >= need
                return jnp.where(ok, mlo, mid), jnp.where(ok, mid, mhi)
            mlo0 = jnp.full((1, qb), -1, I32)
            mhi0 = jnp.full((1, qb), sc_ref.shape[0] - 1, I32)
            return t_hi, lax.fori_loop(0, 12, step, (mlo0, mhi0))[1]

        def no_tie(_):
            return jnp.full((1, qb), jnp.inf, F32), jnp.full((1, qb), _I32_MAX, I32)

        tied = jnp.max(jnp.abs(c_lo - n_sel)) > 0.5
        t_hi, m_sel = lax.cond(tied, tie_search, no_tie, 0)
        return t_lo, t_hi, m_sel

    def all_causal(_):
        return (jnp.full((1, qb), 0.1 * NEG_INF, F32), jnp.full((1, qb), jnp.inf, F32),
                jnp.full((1, qb), _I32_MAX, I32))

    t_lo, t_hi, m_sel = lax.cond((j + 1) * qb > n_sel, select, all_causal, 0)

    grp = DSA_HEADS // DSA_KV_HEADS
    dh = DSA_HEAD_DIM
    qd = qd_ref[0]
    q_groups = [jnp.concatenate([qd[:, (g * grp + i) * dh:(g * grp + i + 1) * dh] for i in range(grp)], axis=0)
                for g in range(DSA_KV_HEADS)]
    posq = posq_ref[0, j]
    acc_ref[...] = jnp.zeros_like(acc_ref)

    def att_chunk(c, carry):
        ms, ls = carry
        off = pl.multiple_of(c * ck, ck)
        x = sc_ref[pl.ds(off, ck), :]
        sel = (x >= t_hi) | ((x >= t_lo) & (off + k_iota <= m_sel))
        dist = jnp.abs(posk_ref[0, pl.ds(off, ck), :] - posq).astype(F32)
        kd = kd_ref[0, pl.ds(off, ck), :]
        new_ms, new_ls = [], []
        s_all = [lax.dot_general(kd[:, g * dh:(g + 1) * dh], q_groups[g], _NT,
                                 preferred_element_type=F32) for g in range(DSA_KV_HEADS)]
        for g in range(DSA_KV_HEADS):
            s = s_all[g]
            ps, alphas = [], []
            for i in range(grp):
                hd = g * grp + i
                slope = 2.0 ** (-8.0 * (hd + 1) / DSA_HEADS) * LOG2E
                si = jnp.where(sel, s[:, i * qb:(i + 1) * qb] - slope * dist, NEG_INF)
                m_new = jnp.maximum(ms[hd], jnp.max(si, axis=0, keepdims=True))
                alpha = jnp.exp2(ms[hd] - m_new)
                p = jnp.exp2(si - m_new)
                new_ls.append(alpha * ls[hd] + jnp.sum(p, axis=0, keepdims=True))
                new_ms.append(m_new)
                ps.append(p.astype(BF16))
                alphas.append(alpha)
            pv = jnp.dot(vdt_ref[0, c, g * dh:(g + 1) * dh, :], jnp.concatenate(ps, axis=1),
                         preferred_element_type=F32)
            acc_ref[g] = jnp.concatenate(alphas, axis=1) * acc_ref[g] + pv
        return tuple(new_ms), tuple(new_ls)

    init = ((jnp.full((1, qb), NEG_INF, F32),) * DSA_HEADS, (jnp.zeros((1, qb), F32),) * DSA_HEADS)
    _, ls = lax.fori_loop(0, nch, att_chunk, init)
    o_t = jnp.concatenate([acc_ref[hd // grp][:, (hd % grp) * qb:(hd % grp + 1) * qb] / ls[hd]
                           for hd in range(DSA_HEADS)], axis=0)
    y_ref[0] = o_t.T.astype(BF16)


def _dsa(qi, kiw, qd, kd, vdt, posk, posq):
    b, s, _ = qd.shape
    qb = QB_DSA
    _, nck, vw, ck = vdt.shape
    grp = DSA_HEADS // DSA_KV_HEADS
    blk = lambda w: pl.BlockSpec((1, qb, w), lambda bi, j: (bi, j, 0))
    seq = lambda w: pl.BlockSpec((1, s, w), lambda bi, j: (bi, 0, 0))
    return pl.pallas_call(
        _dsa_kernel,
        out_shape=jax.ShapeDtypeStruct((b, s, qd.shape[2]), BF16),
        grid=(b, s // qb),
        in_specs=[blk(qi.shape[2]), seq(kiw.shape[2]), blk(qd.shape[2]), seq(kd.shape[2]),
                  pl.BlockSpec((1, nck, vw, ck), lambda bi, j: (bi, 0, 0, 0)),
                  seq(1),
                  pl.BlockSpec((1, s // qb, 1, qb), lambda bi, j: (bi, 0, 0, 0))],
        out_specs=blk(qd.shape[2]),
        scratch_shapes=[pltpu.VMEM((s, qb), F32),
                        pltpu.VMEM((DSA_KV_HEADS, DSA_HEAD_DIM, grp * qb), F32)],
        compiler_params=_cparams(("parallel", "arbitrary")),
        name="dsa",
    )(qi, kiw, qd, kd, vdt, posk, posq)


_META_ROWS = 16


def _post_kernel(ym_ref, yd_ref, x_ref, mod_ref, woa_ref, wob_ref, gpost_ref, gpre_ref, wr_ref, br_ref,
                 x1_ref, h2_ref, meta_ref, gates_ref, cnt_ref):
    i = pl.program_id(0)
    d = x_ref.shape[1]
    tm = x_ref.shape[0]

    @pl.when(i == 0)
    def _():
        cnt_ref[...] = jnp.zeros_like(cnt_ref)

    mix = (jnp.dot(ym_ref[...], woa_ref[...], preferred_element_type=F32)
           + jnp.dot(yd_ref[...], wob_ref[...], preferred_element_type=F32))
    gt1 = mod_ref[0, :, 2 * d:3 * d]
    sh2 = mod_ref[0, :, 3 * d:4 * d]
    sc2 = mod_ref[0, :, 4 * d:5 * d]
    x1 = x_ref[...] + gt1 * _rms(mix, gpost_ref[...])
    x1_ref[...] = x1
    h2 = _rms(x1, gpre_ref[...]) * (1.0 + sc2) + sh2
    h2_ref[...] = h2

    logits = jnp.dot(h2, wr_ref[...], preferred_element_type=F32,
                     precision=lax.Precision.HIGHEST) + br_ref[...]
    ne = logits.shape[1]
    lane = lax.broadcasted_iota(I32, (tm, ne), 1).astype(F32)
    work = logits
    ids, vals = [], []
    for _ in range(TOP_K):
        mx = jnp.max(work, axis=-1, keepdims=True)
        idx = jnp.min(jnp.where(work == mx, lane, float(ne)), axis=-1, keepdims=True)
        ids.append(idx)
        vals.append(mx)
        work = jnp.where(lane == idx, -jnp.inf, work)
    es = [jnp.exp(v - vals[0]) for v in vals]
    den = es[0] + es[1] + es[2] + es[3]
    gates = [e / den for e in es]

    member = jnp.zeros((tm, ne), F32)
    for idx in ids:
        member = member + (lane == idx).astype(F32)
    r_i = lax.broadcasted_iota(I32, (tm, tm), 0)
    c_i = lax.broadcasted_iota(I32, (tm, tm), 1)
    tri = (c_i < r_i).astype(BF16)
    before = jnp.dot(tri, member.astype(BF16), preferred_element_type=F32) + cnt_ref[...]
    cnt_ref[...] = cnt_ref[...] + jnp.sum(member, axis=0, keepdims=True)
    ranks = [jnp.sum(jnp.where(lane == idx, before, 0.0), axis=-1, keepdims=True) for idx in ids]

    lane_m = lax.broadcasted_iota(I32, (tm, LANES), 1)
    rec = jnp.zeros((tm, LANES), F32)
    for k, col in enumerate(ids + gates + ranks):
        rec = jnp.where(lane_m == k, col, rec)
    meta_ref[...] = rec.T[0:_META_ROWS, :]
    lane_g = lax.broadcasted_iota(I32, (tm, TOP_K), 1)
    gm = jnp.zeros((tm, TOP_K), F32)
    for k, col in enumerate(gates):
        gm = jnp.where(lane_g == k, col, gm)
    gates_ref[...] = gm


def _post(ym, yd, x2, mod3, woa, wob, g_post, g_pre, w_router, b_router, tiles_per_seq):
    n, d = x2.shape
    tm = TM_POST
    ne = w_router.shape[1]
    tok = lambda w: pl.BlockSpec((tm, w), lambda i: (i, 0))
    full = lambda a: pl.BlockSpec(a.shape, lambda i: (0,) * a.ndim)
    return pl.pallas_call(
        _post_kernel,
        out_shape=[jax.ShapeDtypeStruct((n, d), F32), jax.ShapeDtypeStruct((n, d), F32),
                   jax.ShapeDtypeStruct((_META_ROWS, n), F32), jax.ShapeDtypeStruct((n, TOP_K), F32)],
        grid=(n // tm,),
        in_specs=[tok(ym.shape[1]), tok(yd.shape[1]), tok(d),
                  pl.BlockSpec((1, 1, mod3.shape[2]), lambda i: (i // tiles_per_seq, 0, 0)),
                  full(woa), full(wob), full(g_post), full(g_pre), full(w_router), full(b_router)],
        out_specs=[tok(d), tok(d), pl.BlockSpec((_META_ROWS, tm), lambda i: (0, i)), tok(TOP_K)],
        scratch_shapes=[pltpu.VMEM((1, ne), F32)],
        compiler_params=_cparams(("arbitrary",)),
        name="post",
    )(ym, yd, x2, mod3, woa, wob, g_post, g_pre, w_router, b_router)


def _route_kernel(meta_ref, dest_ref, blk_ref):
    ne = N_EXPERTS
    nbl = blk_ref.shape[1]
    e_col = lax.broadcasted_iota(I32, (ne, 1), 0).astype(F32)
    onehots = [(meta_ref[k:k + 1, :] == e_col) for k in range(TOP_K)]
    counts = jnp.zeros((ne, 1), F32)
    for oh in onehots:
        counts = counts + jnp.sum(oh.astype(F32), axis=1, keepdims=True)
    padded = jnp.floor((counts + float(BM_FFN - 1)) / BM_FFN) * BM_FFN
    r_i = lax.broadcasted_iota(I32, (ne, ne), 0)
    c_i = lax.broadcasted_iota(I32, (ne, ne), 1)
    tri = (c_i < r_i).astype(F32)
    padded_f = jnp.broadcast_to(padded, (ne, LANES))
    start = jnp.dot(tri, padded_f, preferred_element_type=F32, precision=lax.Precision.HIGHEST)[:, 0:1]
    for k in range(TOP_K):
        add = jnp.sum(jnp.where(onehots[k], start, 0.0), axis=0, keepdims=True)
        dest_ref[k:k + 1, :] = (meta_ref[2 * TOP_K + k:2 * TOP_K + k + 1, :] + add).astype(I32)
    end = start + padded
    row0 = (lax.broadcasted_iota(I32, (1, nbl), 1) * BM_FFN).astype(F32)
    blk_e = jnp.minimum(jnp.sum((end <= row0).astype(F32), axis=0, keepdims=True), float(ne - 1))
    own = blk_e == e_col
    real_end = jnp.sum(jnp.where(own, start + counts, 0.0), axis=0, keepdims=True)
    nvalid = jnp.clip(real_end - row0, 0.0, float(BM_FFN))
    nact = jnp.sum(padded, axis=0, keepdims=True) / BM_FFN
    rows = lax.broadcasted_iota(I32, blk_ref.shape, 0)
    out = jnp.where(rows == 0, blk_e, jnp.where(rows == 1, nvalid, jnp.broadcast_to(nact, blk_ref.shape)))
    blk_ref[...] = out.astype(I32)


def _route(meta, n_blocks):
    n = meta.shape[1]
    nbl = (n_blocks + LANES - 1) // LANES * LANES
    return pl.pallas_call(
        _route_kernel,
        out_shape=[jax.ShapeDtypeStruct((TOP_K, n), I32), jax.ShapeDtypeStruct((8, nbl), I32)],
        compiler_params=pltpu.CompilerParams(vmem_limit_bytes=VMEM_LIMIT),
        name="route",
    )(meta)


def _dispatch_kernel(dest_ref, h_ref, zero_ref, xs_ref, sem):
    del zero_ref
    i = pl.program_id(0)
    tm = h_ref.shape[0]
    n = pl.num_programs(0) * tm

    def row_copy(t, k):
        dst = dest_ref[k * n + i * tm + t]
        return pltpu.make_async_copy(h_ref.at[pl.ds(t, 1), :], xs_ref.at[pl.ds(dst, 1), :], sem)

    def start(t, c):
        for k in range(TOP_K):
            row_copy(t, k).start()
        return c

    def wait(t, c):
        for k in range(TOP_K):
            row_copy(t, k).wait()
        return c

    lax.fori_loop(0, tm, start, 0, unroll=8)
    lax.fori_loop(0, tm, wait, 0, unroll=8)


def _dispatch(dest_flat, h2, n_slots):
    n, d = h2.shape
    tm = TM_ROWS
    return pl.pallas_call(
        _dispatch_kernel,
        out_shape=jax.ShapeDtypeStruct((n_slots, d), h2.dtype),
        grid_spec=pltpu.PrefetchScalarGridSpec(
            num_scalar_prefetch=1,
            grid=(n // tm,),
            in_specs=[pl.BlockSpec((tm, d), lambda i, dest: (i, 0)),
                      pl.BlockSpec(memory_space=pl.ANY)],
            out_specs=pl.BlockSpec(memory_space=pl.ANY),
            scratch_shapes=[pltpu.SemaphoreType.DMA],
        ),
        input_output_aliases={2: 0},
        compiler_params=_cparams(("arbitrary",)),
        name="dispatch",
    )(dest_flat, h2, jnp.zeros((n_slots, d), h2.dtype))


def _split_w1_kernel(w_ref, g_ref, l_ref, t_ref):
    _, d, cw = w_ref.shape
    for r in range(d // LANES):
        rows = slice(r * LANES, (r + 1) * LANES)
        t_ref[r] = w_ref[0, rows, :].T
        g_ref[0, rows, :] = t_ref[r, pl.ds(0, cw // 2, stride=2), :].T.astype(BF16)
        l_ref[0, rows, :] = t_ref[r, pl.ds(1, cw // 2, stride=2), :].T.astype(BF16)


def _split_w1(w1):
    e, d, f2 = w1.shape
    cw = 1024
    out = jax.ShapeDtypeStruct((e, d, f2 // 2), BF16)
    return pl.pallas_call(
        _split_w1_kernel,
        out_shape=[out, out],
        grid=(e, f2 // cw),
        in_specs=[pl.BlockSpec((1, d, cw), lambda i, j: (i, 0, j))],
        out_specs=[pl.BlockSpec((1, d, cw // 2), lambda i, j: (i, 0, j))] * 2,
        scratch_shapes=[pltpu.VMEM((d // LANES, cw, LANES), F32)],
        compiler_params=_cparams(("parallel", "arbitrary")),
        name="split_w1",
    )(w1)


def _ffn_kernel(be_ref, nv_ref, na_ref, xs_ref, w1g_ref, w1l_ref, w2_ref, b1g_ref, b1l_ref, b2_ref, y_ref,
                w2b_ref):
    b = pl.program_id(0)

    @pl.when((b < na_ref[0]) & ((b == 0) | (be_ref[b] != be_ref[jnp.maximum(b - 1, 0)])))
    def _():
        w2b_ref[...] = w2_ref[0].astype(BF16)

    @pl.when(b < na_ref[0])
    def _():
        bm = xs_ref.shape[0]
        rows = lax.broadcasted_iota(I32, (bm, 1), 0)
        x = jnp.where(rows < nv_ref[b], xs_ref[...], 0.0).astype(BF16)
        glu = jnp.dot(x, w1g_ref[0], preferred_element_type=F32) + b1g_ref[0]
        lin = jnp.dot(x, w1l_ref[0], preferred_element_type=F32) + b1l_ref[0]
        glu = jnp.minimum(glu, SWIGLU_LIMIT)
        lin = jnp.clip(lin, -SWIGLU_LIMIT, SWIGLU_LIMIT)
        act = glu * jax.nn.sigmoid(SWIGLU_ALPHA * glu) * (lin + 1.0)
        y_ref[...] = jnp.dot(act.astype(BF16), w2b_ref[...], preferred_element_type=F32) + b2_ref[0]

    @pl.when(b >= na_ref[0])
    def _():
        y_ref[...] = jnp.zeros_like(y_ref)


def _ffn(blk_e, blk_nv, n_act, xs, w1g, w1l, w2, b1g, b1l, b2):
    n_slots, d = xs.shape
    bm = BM_FFN
    f = w1g.shape[2]
    blk = lambda b, be, nv, na: (jnp.minimum(b, na[0] - 1), 0)
    exp3 = lambda b, be, nv, na: (be[jnp.minimum(b, na[0] - 1)], 0, 0)
    return pl.pallas_call(
        _ffn_kernel,
        out_shape=jax.ShapeDtypeStruct((n_slots, d), F32),
        grid_spec=pltpu.PrefetchScalarGridSpec(
            num_scalar_prefetch=3,
            grid=(n_slots // bm,),
            in_specs=[pl.BlockSpec((bm, d), blk),
                      pl.BlockSpec((1, d, f), exp3), pl.BlockSpec((1, d, f), exp3),
                      pl.BlockSpec((1, f, d), exp3),
                      pl.BlockSpec((1, 1, f), exp3), pl.BlockSpec((1, 1, f), exp3),
                      pl.BlockSpec((1, 1, d), exp3)],
            out_specs=pl.BlockSpec((bm, d), lambda b, be, nv, na: (b, 0)),
            scratch_shapes=[pltpu.VMEM((f, d), BF16)],
        ),
        compiler_params=_cparams(("arbitrary",)),
        name="ffn",
    )(blk_e, blk_nv, n_act, xs, w1g, w1l, w2, b1g, b1l, b2)


def _combine_kernel(dest_ref, y_ref, gates_ref, x1_ref, mod_ref, gpost_ref, o_ref, buf_ref, sem):
    i = pl.program_id(0)
    tm, d = x1_ref.shape
    n = pl.num_programs(0) * tm

    def row_copy(t, k):
        src = dest_ref[k * n + i * tm + t]
        return pltpu.make_async_copy(y_ref.at[pl.ds(src, 1), :], buf_ref.at[k, pl.ds(t, 1), :], sem)

    def start(t, c):
        for k in range(TOP_K):
            row_copy(t, k).start()
        return c

    def wait(t, c):
        for k in range(TOP_K):
            row_copy(t, k).wait()
        return c

    lax.fori_loop(0, tm, start, 0, unroll=8)
    lax.fori_loop(0, tm, wait, 0, unroll=8)
    g = gates_ref[...]
    ffn = g[:, 0:1] * buf_ref[0]
    for k in range(1, TOP_K):
        ffn = ffn + g[:, k:k + 1] * buf_ref[k]
    gt2 = mod_ref[0, :, 5 * d:6 * d]
    o_ref[...] = x1_ref[...] + gt2 * _rms(ffn, gpost_ref[...])


def _combine(dest_flat, y, gates, x1, mod3, g_post, tiles_per_seq):
    n, d = x1.shape
    tm = TM_ROWS
    tok = lambda w: pl.BlockSpec((tm, w), lambda i, dest: (i, 0))
    return pl.pallas_call(
        _combine_kernel,
        out_shape=jax.ShapeDtypeStruct((n, d), F32),
        grid_spec=pltpu.PrefetchScalarGridSpec(
            num_scalar_prefetch=1,
            grid=(n // tm,),
            in_specs=[pl.BlockSpec(memory_space=pl.ANY), tok(TOP_K), tok(d),
                      pl.BlockSpec((1, 1, mod3.shape[2]), lambda i, dest: (i // tiles_per_seq, 0, 0)),
                      pl.BlockSpec(g_post.shape, lambda i, dest: (0, 0))],
            out_specs=tok(d),
            scratch_shapes=[pltpu.VMEM((TOP_K, tm, d), F32), pltpu.SemaphoreType.DMA],
        ),
        compiler_params=_cparams(("arbitrary",)),
        name="combine",
    )(dest_flat, y, gates, x1, mod3, g_post)


def _layer(x, mod, positions, g_pre_mix, g_post_mix, g_pre_ffn, g_post_ffn, w_in, g_q_a, g_kv_a,
           w_q_b, w_kv_b, w_o, w_router, b_router, w_mlp1, b_mlp1, w_mlp2, b_mlp2):
    b, s, d = x.shape
    n = b * s
    mod3 = mod.reshape(b, 1, mod.shape[1])
    row = lambda g: g.reshape(1, -1)

    win, wq, wkv = _relayout_in_weights(w_in, w_q_b, w_kv_b)
    qm, km, vm, qd, kd, vd, qi, kiw = _inproj(
        x, mod3, positions.reshape(b, s, 1), row(g_pre_mix), win, row(g_q_a), row(g_kv_a), wq, wkv,
        _rope_rows())
    y_mla = _mla(qm, km, vm)
    y_dsa = _dsa(qi, kiw, qd, kd, vd, positions.reshape(b, s, 1),
                 positions.reshape(b, s // QB_DSA, 1, QB_DSA))

    wo = w_o.astype(BF16)
    split = MLA_HEADS * MLA_V
    x1, h2, meta, gates = _post(
        y_mla.reshape(n, -1), y_dsa.reshape(n, -1), x.reshape(n, d), mod3, wo[:split], wo[split:],
        row(g_post_mix), row(g_pre_ffn), w_router, row(b_router), s // TM_POST)

    n_slots = n * TOP_K + N_EXPERTS * BM_FFN
    dest, blk = _route(meta, n_slots // BM_FFN)
    dest_flat = dest.reshape(-1)
    xs = _dispatch(dest_flat, h2, n_slots)
    f = w_mlp2.shape[1]
    w1g, w1l = _split_w1(w_mlp1)
    y = _ffn(blk[0], blk[1], blk[2, 0:1], xs, w1g, w1l, w_mlp2,
             b_mlp1[:, 0::2].reshape(-1, 1, f), b_mlp1[:, 1::2].reshape(-1, 1, f),
             b_mlp2.reshape(-1, 1, d))
    out = _combine(dest_flat, y, gates, x1, mod3, row(g_post_ffn), s // TM_ROWS)
    return out.reshape(b, s, d)


def kernel(x, c, positions, w_ada, b_ada, g_pre_mix, g_post_mix, g_pre_ffn, g_post_ffn, w_in, g_q_a, g_kv_a, w_q_b, w_kv_b, w_o, w_router, b_router, w_mlp1, b_mlp1, w_mlp2, b_mlp2):
    for l in range(w_ada.shape[0]):
        mod = _ada(c, w_ada[l], b_ada[l])
        x = _layer(x, mod, positions, g_pre_mix[l], g_post_mix[l], g_pre_ffn[l], g_post_ffn[l],
                   w_in[l], g_q_a[l], g_kv_a[l], w_q_b[l], w_kv_b[l], w_o[l], w_router[l], b_router[l],
                   w_mlp1[l], b_mlp1[l], w_mlp2[l], b_mlp2[l])
    return x
```

```python
import functools

import jax
import jax.numpy as jnp
from jax import lax
from jax.experimental import pallas as pl
from jax.experimental.pallas import tpu as pltpu

F32 = jnp.float32
BF16 = jnp.bfloat16
I32 = jnp.int32

MLA_HEADS = 8
MLA_Q_RANK = 256
MLA_KV_RANK = 128
MLA_NOPE = 64
MLA_ROPE = 32
MLA_V = 64
ROPE_THETA = 10000.0
DSA_HEADS = 8
DSA_KV_HEADS = 2
DSA_HEAD_DIM = 64
IDX_HEADS = 8
IDX_DIM = 32
IDX_TOPK_MAX = 256
N_EXPERTS = 32
TOP_K = 4
SWIGLU_ALPHA = 1.702
SWIGLU_LIMIT = 7.0
NORM_EPS = 1e-6
NEG_INF = -1e30
LOG2E = 1.4426950408889634

LANES = 128
HEAD_PAD = 128
VMEM_LIMIT = 52 * 1024 * 1024

TM_PROJ = 512
KV_CHUNK = 256
TQ_MLA = 512
QB_DSA = 128
TM_POST = 512
BM_FFN = 256
TM_ROWS = 256

_NT = (((1,), (1,)), ((), ()))


def _rms(x, g):
    ms = jnp.mean(x * x, axis=-1, keepdims=True)
    return x * lax.rsqrt(ms + NORM_EPS) * g


def _cparams(sem, vmem=VMEM_LIMIT):
    return pltpu.CompilerParams(dimension_semantics=sem, vmem_limit_bytes=vmem)


def _ada_kernel(c_ref, w_ref, b_ref, o_ref):
    c = c_ref[...]
    cond = c * jax.nn.sigmoid(c)
    o_ref[...] = jnp.dot(cond.astype(BF16), w_ref[...].astype(BF16),
                         preferred_element_type=F32) + b_ref[...]


def _ada(c, w_ada, b_ada):
    b, d = c.shape
    n = w_ada.shape[1]
    tn = 1024
    return pl.pallas_call(
        _ada_kernel,
        out_shape=jax.ShapeDtypeStruct((b, n), F32),
        grid=(n // tn,),
        in_specs=[pl.BlockSpec((b, d), lambda i: (0, 0)),
                  pl.BlockSpec((d, tn), lambda i: (0, i)),
                  pl.BlockSpec((1, tn), lambda i: (0, i))],
        out_specs=pl.BlockSpec((b, tn), lambda i: (0, i)),
        compiler_params=_cparams(("arbitrary",)),
        name="ada",
    )(c, w_ada, b_ada.reshape(1, n))


_O_CQ = 0
_O_CKV = _O_CQ + MLA_Q_RANK
_O_KRA = _O_CKV + MLA_KV_RANK
_O_KRB = _O_KRA + HEAD_PAD
_O_QD = _O_KRB + HEAD_PAD
_O_KD = _O_QD + DSA_HEADS * DSA_HEAD_DIM
_O_VD = _O_KD + DSA_KV_HEADS * DSA_HEAD_DIM
_O_QI = _O_VD + DSA_KV_HEADS * DSA_HEAD_DIM
_O_KIW = _O_QI + IDX_HEADS * IDX_DIM
_W_IN = _O_KIW + LANES


def _inproj_kernel(x_ref, mod_ref, pos_ref, gpre_ref, win_ref, gq_ref, gkv_ref, wq_ref, wkv_ref,
                   rope_ref, qm_ref, km_ref, vm_ref, qd_ref, kd_ref, vd_ref, qi_ref, kiw_ref):
    d = x_ref.shape[2]
    x = x_ref[0]
    sh1 = mod_ref[0, :, 0:d]
    sc1 = mod_ref[0, :, d:2 * d]
    h = _rms(x, gpre_ref[...]) * (1.0 + sc1) + sh1
    proj = jnp.dot(h.astype(BF16), win_ref[...], preferred_element_type=F32)

    ang = pos_ref[0].astype(F32) * rope_ref[0:1, :]
    cos_t = jnp.cos(ang) * rope_ref[1:2, :] + rope_ref[3:4, :]
    sin_t = jnp.sin(ang) * rope_ref[2:3, :]

    hw = MLA_HEADS * HEAD_PAD
    nq = _rms(proj[:, _O_CQ:_O_CQ + MLA_Q_RANK], gq_ref[...])
    qab = jnp.dot(nq.astype(BF16), wq_ref[...], preferred_element_type=F32)
    scale = (MLA_NOPE + MLA_ROPE) ** -0.5 * LOG2E
    nkv = _rms(proj[:, _O_CKV:_O_CKV + MLA_KV_RANK], gkv_ref[...])
    kv = jnp.dot(nkv.astype(BF16), wkv_ref[...], preferred_element_type=F32)
    kr = proj[:, _O_KRA:_O_KRA + HEAD_PAD] * cos_t + proj[:, _O_KRB:_O_KRB + HEAD_PAD] * sin_t
    for hd in range(MLA_HEADS):
        sl = slice(hd * HEAD_PAD, (hd + 1) * HEAD_PAD)
        slb = slice(hw + hd * HEAD_PAD, hw + (hd + 1) * HEAD_PAD)
        qm_ref[0, :, sl] = ((qab[:, sl] * cos_t + qab[:, slb] * sin_t) * scale).astype(BF16)
        km_ref[0, :, sl] = (kv[:, sl] + kr).astype(BF16)

    qd_ref[0] = (proj[:, _O_QD:_O_KD] * (DSA_HEAD_DIM ** -0.5 * LOG2E)).astype(BF16)
    kd_ref[0] = proj[:, _O_KD:_O_VD].astype(BF16)
    qi_ref[0] = (proj[:, _O_QI:_O_KIW] * (IDX_DIM ** -0.5)).astype(BF16)
    kiw_ref[0] = proj[:, _O_KIW:_W_IN]
    ck = vm_ref.shape[3]
    for t in range(x_ref.shape[1] // ck):
        rows = slice(t * ck, (t + 1) * ck)
        vm_ref[0, t] = kv[rows, hw:hw + MLA_HEADS * MLA_V].T.astype(BF16)
        vd_ref[0, t] = proj[rows, _O_VD:_O_QI].T.astype(BF16)


def _relayout_in_weights(w_in, w_q_b, w_kv_b):
    d = w_in.shape[0]
    half = MLA_ROPE // 2
    o = 0
    segs = {}
    for name, width in (("cq", MLA_Q_RANK), ("ckv", MLA_KV_RANK), ("kr", MLA_ROPE),
                        ("qd", DSA_HEADS * DSA_HEAD_DIM), ("kd", DSA_KV_HEADS * DSA_HEAD_DIM),
                        ("vd", DSA_KV_HEADS * DSA_HEAD_DIM), ("qi", IDX_HEADS * IDX_DIM),
                        ("ki", IDX_DIM), ("wi", IDX_HEADS)):
        segs[name] = w_in[:, o:o + width]
        o += width
    z = lambda n: jnp.zeros((d, n), w_in.dtype)
    x1, x2 = segs["kr"][:, :half], segs["kr"][:, half:]
    tail = HEAD_PAD - MLA_NOPE - MLA_ROPE
    kra = jnp.concatenate([z(MLA_NOPE), x1, x2, z(tail)], axis=1)
    krb = jnp.concatenate([z(MLA_NOPE), x2, x1, z(tail)], axis=1)
    win = jnp.concatenate([segs["cq"], segs["ckv"], kra, krb, segs["qd"], segs["kd"], segs["vd"],
                           segs["qi"], segs["ki"], segs["wi"], z(LANES - IDX_DIM - IDX_HEADS)], axis=1)
    r = w_q_b.shape[0]
    wq = w_q_b.reshape(r, MLA_HEADS, MLA_NOPE + MLA_ROPE)
    zq = lambda n: jnp.zeros((r, MLA_HEADS, n), w_q_b.dtype)
    wqa = jnp.concatenate([wq, zq(tail)], axis=2).reshape(r, MLA_HEADS * HEAD_PAD)
    wqb = jnp.concatenate([zq(MLA_NOPE), wq[:, :, MLA_NOPE + half:], wq[:, :, MLA_NOPE:MLA_NOPE + half],
                           zq(tail)], axis=2).reshape(r, MLA_HEADS * HEAD_PAD)
    rk = w_kv_b.shape[0]
    wkv = w_kv_b.reshape(rk, MLA_HEADS, MLA_NOPE + MLA_V)
    wkn = jnp.concatenate([wkv[:, :, :MLA_NOPE], jnp.zeros((rk, MLA_HEADS, HEAD_PAD - MLA_NOPE), w_kv_b.dtype)],
                          axis=2).reshape(rk, MLA_HEADS * HEAD_PAD)
    wv = wkv[:, :, MLA_NOPE:].reshape(rk, MLA_HEADS * MLA_V)
    return (win.astype(BF16), jnp.concatenate([wqa, wqb], axis=1).astype(BF16),
            jnp.concatenate([wkn, wv], axis=1).astype(BF16))


def _rope_rows():
    half = MLA_ROPE // 2
    lane = jnp.arange(LANES)
    freqs = ROPE_THETA ** (-jnp.arange(half, dtype=F32) / half)
    in_x1 = (lane >= MLA_NOPE) & (lane < MLA_NOPE + half)
    in_x2 = (lane >= MLA_NOPE + half) & (lane < MLA_NOPE + MLA_ROPE)
    fr = jnp.where(in_x1 | in_x2, freqs[(lane - MLA_NOPE) % half], 0.0)
    cosm = (in_x1 | in_x2).astype(F32)
    sinm = jnp.where(in_x1, -1.0, jnp.where(in_x2, 1.0, 0.0))
    nopem = (lane < MLA_NOPE).astype(F32)
    rows = jnp.stack([fr, cosm, sinm, nopem], axis=0).astype(F32)
    return jnp.concatenate([rows, jnp.zeros((4, LANES), F32)], axis=0)


def _inproj(x, mod3, pos3, g_pre, win, g_q, g_kv, wq, wkv, rope_rows):
    b, s, d = x.shape
    tm = TM_PROJ
    ck = KV_CHUNK
    tok = lambda w: pl.BlockSpec((1, tm, w), lambda bi, i: (bi, i, 0))
    full = lambda a: pl.BlockSpec(a.shape, lambda bi, i: (0,) * a.ndim)
    tr = lambda w: pl.BlockSpec((1, tm // ck, w, ck), lambda bi, i: (bi, i, 0, 0))
    tok_out = lambda w, dt: (jax.ShapeDtypeStruct((b, s, w), dt), tok(w))
    tr_out = lambda w: (jax.ShapeDtypeStruct((b, s // ck, w, ck), BF16), tr(w))
    outs = [tok_out(MLA_HEADS * HEAD_PAD, BF16), tok_out(MLA_HEADS * HEAD_PAD, BF16),
            tr_out(MLA_HEADS * MLA_V), tok_out(DSA_HEADS * DSA_HEAD_DIM, BF16),
            tok_out(DSA_KV_HEADS * DSA_HEAD_DIM, BF16), tr_out(DSA_KV_HEADS * DSA_HEAD_DIM),
            tok_out(IDX_HEADS * IDX_DIM, BF16), tok_out(LANES, F32)]
    return pl.pallas_call(
        _inproj_kernel,
        out_shape=[o[0] for o in outs],
        grid=(b, s // tm),
        in_specs=[tok(d),
                  pl.BlockSpec((1, 1, mod3.shape[2]), lambda bi, i: (bi, 0, 0)),
                  tok(1), full(g_pre), full(win), full(g_q), full(g_kv), full(wq), full(wkv),
                  full(rope_rows)],
        out_specs=[o[1] for o in outs],
        compiler_params=_cparams(("parallel", "arbitrary")),
        name="inproj",
    )(x, mod3, pos3, g_pre, win, g_q, g_kv, wq, wkv, rope_rows)


def _mla_kernel(q_ref, k_ref, vt_ref, o_ref, m_ref, l_ref, acc_ref):
    j = pl.program_id(1)
    tq = q_ref.shape[1]
    tk = k_ref.shape[1] // vt_ref.shape[1]
    m_ref[...] = jnp.full_like(m_ref, NEG_INF)
    l_ref[...] = jnp.zeros_like(l_ref)
    acc_ref[...] = jnp.zeros_like(acc_ref)
    per_q = tq // tk
    krow = lax.broadcasted_iota(I32, (tk, 1), 0)
    qcol = j * tq + lax.broadcasted_iota(I32, (1, tq), 1)

    def chunk(c, diagonal):
        off = pl.multiple_of(c * tk, tk)
        m_all = m_ref[...]
        l_all = l_ref[...]
        accs = [acc_ref[hd] for hd in range(MLA_HEADS)]
        ms, ls = [], []

        def scores(hd):
            lanes = slice(hd * HEAD_PAD, (hd + 1) * HEAD_PAD)
            return lax.dot_general(k_ref[0, pl.ds(off, tk), lanes], q_ref[0, :, lanes], _NT,
                                   preferred_element_type=F32)

        s_next = scores(0)
        pend = None
        for hd in range(MLA_HEADS):
            s = s_next
            if hd + 1 < MLA_HEADS:
                s_next = scores(hd + 1)
            if diagonal:
                s = jnp.where(off + krow <= qcol, s, NEG_INF)
            m_old = m_all[hd:hd + 1, :]
            m_new = jnp.maximum(m_old, jnp.max(s, axis=0, keepdims=True))
            alpha = jnp.exp2(m_old - m_new)
            p = jnp.exp2(s - m_new)
            ls.append(alpha * l_all[hd:hd + 1, :] + jnp.sum(p, axis=0, keepdims=True))
            ms.append(m_new)
            pv = jnp.dot(vt_ref[0, c, hd * MLA_V:(hd + 1) * MLA_V, :], p.astype(BF16),
                         preferred_element_type=F32)
            if pend is not None:
                accs[pend[0]] = pend[1] * accs[pend[0]] + pend[2]
            pend = (hd, alpha, pv)
        accs[pend[0]] = pend[1] * accs[pend[0]] + pend[2]
        m_ref[...] = jnp.concatenate(ms, axis=0)
        l_ref[...] = jnp.concatenate(ls, axis=0)
        for hd in range(MLA_HEADS):
            acc_ref[hd] = accs[hd]

    def body(c, carry):
        chunk(c, False)
        return carry

    lax.fori_loop(0, j * per_q, body, 0)
    for t in range(per_q):
        chunk(j * per_q + t, True)
    o_t = jnp.concatenate([acc_ref[hd] / l_ref[hd:hd + 1, :] for hd in range(MLA_HEADS)], axis=0)
    o_ref[0] = o_t.T.astype(BF16)


def _mla(qm, km, vmt):
    b, s, hw = qm.shape
    tq = TQ_MLA
    _, nck, vw, ck = vmt.shape
    return pl.pallas_call(
        _mla_kernel,
        out_shape=jax.ShapeDtypeStruct((b, s, vw), BF16),
        grid=(b, s // tq),
        in_specs=[pl.BlockSpec((1, tq, hw), lambda bi, j: (bi, j, 0)),
                  pl.BlockSpec((1, s, hw), lambda bi, j: (bi, 0, 0)),
                  pl.BlockSpec((1, nck, vw, ck), lambda bi, j: (bi, 0, 0, 0))],
        out_specs=pl.BlockSpec((1, tq, vw), lambda bi, j: (bi, j, 0)),
        scratch_shapes=[pltpu.VMEM((MLA_HEADS, tq), F32), pltpu.VMEM((MLA_HEADS, tq), F32),
                        pltpu.VMEM((MLA_HEADS, MLA_V, tq), F32)],
        compiler_params=_cparams(("parallel", "arbitrary")),
        name="mla",
    )(qm, km, vmt)


_KEY_NEG_INF = -2139095041
_KEY_POS_INF = 2139095040
_I32_MAX = 2147483647


def _key_to_f32(k):
    bits = k ^ ((k >> 31) & _I32_MAX)
    return lax.bitcast_convert_type(bits, F32)


def _dsa_kernel(qi_ref, kiw_ref, qd_ref, kd_ref, vdt_ref, posk_ref, posq_ref, y_ref, sc_ref, acc_ref):
    j = pl.program_id(1)
    qb, ck = QB_DSA, KV_CHUNK
    nch = (j * qb + qb + ck - 1) // ck
    n_sel = IDX_TOPK_MAX

    kiw_q = kiw_ref[0, pl.ds(pl.multiple_of(j * qb, qb), qb), :]
    w_t = kiw_q.T[IDX_DIM:IDX_DIM + IDX_HEADS, :] * (IDX_HEADS ** -0.5)
    qi = qi_ref[0]
    qi_stack = jnp.concatenate([qi[:, hd * IDX_DIM:(hd + 1) * IDX_DIM] for hd in range(IDX_HEADS)], axis=0)
    q_idx = j * qb + lax.broadcasted_iota(I32, (1, qb), 1)
    k_iota = lax.broadcasted_iota(I32, (ck, 1), 0)

    def idx_chunk(c, carry):
        off = pl.multiple_of(c * ck, ck)
        ki = kiw_ref[0, pl.ds(off, ck), :][:, 0:IDX_DIM].astype(BF16)
        r = lax.dot_general(ki, qi_stack, _NT, preferred_element_type=F32)
        acc = jnp.zeros((ck, qb), F32)
        for hd in range(IDX_HEADS):
            acc = acc + w_t[hd:hd + 1, :] * jnp.maximum(r[:, hd * qb:(hd + 1) * qb], 0.0)
        sc_ref[pl.ds(off, ck), :] = jnp.where(off + k_iota <= q_idx, acc, NEG_INF)
        return carry

    lax.fori_loop(0, nch, idx_chunk, 0)

    n_part = 4

    def count(pred_fn):
        def body(c, parts):
            off = pl.multiple_of(c * ck, ck)
            hit = pred_fn(sc_ref[pl.ds(off, ck), :], off).astype(F32)
            rows = ck // n_part
            return tuple(p + jnp.sum(hit[i * rows:(i + 1) * rows].reshape(rows // 8, 8, qb), axis=0)
                         for i, p in enumerate(parts))
        parts = lax.fori_loop(0, nch, body, (jnp.zeros((8, qb), F32),) * n_part)
        return jnp.sum((parts[0] + parts[1]) + (parts[2] + parts[3]), axis=0, keepdims=True)

    def no_tie(_):
        return jnp.full((1, qb), jnp.inf, F32), jnp.full((1, qb), _I32_MAX, I32)

    def select(_):
        def bisect(_, st):
            lo, hi, c_lo = st
            mid = (lo & hi) + ((lo ^ hi) >> 1)
            t = _key_to_f32(mid)
            cnt = count(lambda x, off: x >= t)
            ok = cnt >= n_sel
            return jnp.where(ok, mid, lo), jnp.where(ok, hi, mid), jnp.where(ok, cnt, c_lo)

        per_round = 4

        def round_(st):
            it, lo, hi, c_lo = st
            lo, hi, c_lo = lax.fori_loop(0, per_round, bisect, (lo, hi, c_lo))
            return it + per_round, lo, hi, c_lo

        def unsettled(st):
            it, _, _, c_lo = st
            return (it < 32) & (jnp.max(jnp.abs(c_lo - n_sel)) > 0.5)

        st0 = (jnp.int32(0), jnp.full((1, qb), _KEY_NEG_INF, I32), jnp.full((1, qb), _KEY_POS_INF, I32),
               jnp.broadcast_to((nch * ck).astype(F32), (1, qb)))
        _, lo, _, c_lo = lax.while_loop(unsettled, round_, st0)
        t_lo = _key_to_f32(lo)

        def tie_search(_):
            t_hi = _key_to_f32(lo + 1)
            need = n_sel - count(lambda x, off: x >= t_hi)

            def step(_, lohi):
                mlo, mhi = lohi
                mid = (mlo + mhi) >> 1
                cnt = count(lambda x, off: (x >= t_lo) & jnp.logical_not(x >= t_hi)
                            & (off + k_iota <= mid))
                ok = cnt >= need
                return jnp.where(ok, mlo, mid), jnp.where(ok, mid, mhi)
            mlo0 = jnp.full((1, qb), -1, I32)
            mhi0 = jnp.full((1, qb), sc_ref.shape[0] - 1, I32)
            return t_hi, lax.fori_loop(0, 12, step, (mlo0, mhi0))[1]

        tied = jnp.max(jnp.abs(c_lo - n_sel)) > 0.5
        t_hi, m_sel = lax.cond(tied, tie_search, no_tie, 0)
        return t_lo, t_hi, m_sel

    def all_causal(_):
        return (jnp.full((1, qb), 0.1 * NEG_INF, F32), jnp.full((1, qb), jnp.inf, F32),
                jnp.full((1, qb), _I32_MAX, I32))

    t_lo, t_hi, m_sel = lax.cond((j + 1) * qb > n_sel, select, all_causal, 0)

    grp = DSA_HEADS // DSA_KV_HEADS
    dh = DSA_HEAD_DIM
    qd = qd_ref[0]
    q_groups = [jnp.concatenate([qd[:, (g * grp + i) * dh:(g * grp + i + 1) * dh] for i in range(grp)], axis=0)
                for g in range(DSA_KV_HEADS)]
    posq = posq_ref[0, j]
    acc_ref[...] = jnp.zeros_like(acc_ref)

    def att_chunk(c, carry):
        ms, ls = carry
        off = pl.multiple_of(c * ck, ck)
        x = sc_ref[pl.ds(off, ck), :]
        sel = (x >= t_hi) | ((x >= t_lo) & (off + k_iota <= m_sel))
        dist = jnp.abs(posk_ref[0, pl.ds(off, ck), :] - posq).astype(F32)
        kd = kd_ref[0, pl.ds(off, ck), :]
        new_ms, new_ls = [], []
        s_all = [lax.dot_general(kd[:, g * dh:(g + 1) * dh], q_groups[g], _NT,
                                 preferred_element_type=F32) for g in range(DSA_KV_HEADS)]
        for g in range(DSA_KV_HEADS):
            s = s_all[g]
            ps, alphas = [], []
            for i in range(grp):
                hd = g * grp + i
                slope = 2.0 ** (-8.0 * (hd + 1) / DSA_HEADS) * LOG2E
                si = jnp.where(sel, s[:, i * qb:(i + 1) * qb] - slope * dist, NEG_INF)
                m_new = jnp.maximum(ms[hd], jnp.max(si, axis=0, keepdims=True))
                alpha = jnp.exp2(ms[hd] - m_new)
                p = jnp.exp2(si - m_new)
                new_ls.append(alpha * ls[hd] + jnp.sum(p, axis=0, keepdims=True))
                new_ms.append(m_new)
                ps.append(p.astype(BF16))
                alphas.append(alpha)
            pv = jnp.dot(vdt_ref[0, c, g * dh:(g + 1) * dh, :], jnp.concatenate(ps, axis=1),
                         preferred_element_type=F32)
            acc_ref[g] = jnp.concatenate(alphas, axis=1) * acc_ref[g] + pv
        return tuple(new_ms), tuple(new_ls)

    init = ((jnp.full((1, qb), NEG_INF, F32),) * DSA_HEADS, (jnp.zeros((1, qb), F32),) * DSA_HEADS)
    _, ls = lax.fori_loop(0, nch, att_chunk, init)
    o_t = jnp.concatenate([acc_ref[hd // grp][:, (hd % grp) * qb:(hd % grp + 1) * qb] / ls[hd]
                           for hd in range(DSA_HEADS)], axis=0)
    y_ref[0] = o_t.T.astype(BF16)


def _dsa(qi, kiw, qd, kd, vdt, posk, posq):
    b, s, _ = qd.shape
    qb = QB_DSA
    _, nck, vw, ck = vdt.shape
    grp = DSA_HEADS // DSA_KV_HEADS
    blk = lambda w: pl.BlockSpec((1, qb, w), lambda bi, j: (bi, j, 0))
    seq = lambda w: pl.BlockSpec((1, s, w), lambda bi, j: (bi, 0, 0))
    return pl.pallas_call(
        _dsa_kernel,
        out_shape=jax.ShapeDtypeStruct((b, s, qd.shape[2]), BF16),
        grid=(b, s // qb),
        in_specs=[blk(qi.shape[2]), seq(kiw.shape[2]), blk(qd.shape[2]), seq(kd.shape[2]),
                  pl.BlockSpec((1, nck, vw, ck), lambda bi, j: (bi, 0, 0, 0)),
                  seq(1),
                  pl.BlockSpec((1, s // qb, 1, qb), lambda bi, j: (bi, 0, 0, 0))],
        out_specs=blk(qd.shape[2]),
        scratch_shapes=[pltpu.VMEM((s, qb), F32),
                        pltpu.VMEM((DSA_KV_HEADS, DSA_HEAD_DIM, grp * qb), F32)],
        compiler_params=_cparams(("parallel", "arbitrary")),
        name="dsa",
    )(qi, kiw, qd, kd, vdt, posk, posq)


_META_ROWS = 16


def _post_kernel(ym_ref, yd_ref, x_ref, mod_ref, woa_ref, wob_ref, gpost_ref, gpre_ref, wr_ref, br_ref,
                 x1_ref, h2_ref, meta_ref, gates_ref, cnt_ref):
    i = pl.program_id(0)
    d = x_ref.shape[1]
    tm = x_ref.shape[0]

    @pl.when(i == 0)
    def _():
        cnt_ref[...] = jnp.zeros_like(cnt_ref)

    mix = (jnp.dot(ym_ref[...], woa_ref[...], preferred_element_type=F32)
           + jnp.dot(yd_ref[...], wob_ref[...], preferred_element_type=F32))
    gt1 = mod_ref[0, :, 2 * d:3 * d]
    sh2 = mod_ref[0, :, 3 * d:4 * d]
    sc2 = mod_ref[0, :, 4 * d:5 * d]
    x1 = x_ref[...] + gt1 * _rms(mix, gpost_ref[...])
    x1_ref[...] = x1
    h2 = _rms(x1, gpre_ref[...]) * (1.0 + sc2) + sh2
    h2_ref[...] = h2.reshape(h2_ref.shape)

    logits = jnp.dot(h2, wr_ref[...], preferred_element_type=F32,
                     precision=lax.Precision.HIGHEST) + br_ref[...]
    ne = logits.shape[1]
    lane = lax.broadcasted_iota(I32, (tm, ne), 1).astype(F32)
    work = logits
    ids, vals = [], []
    for _ in range(TOP_K):
        mx = jnp.max(work, axis=-1, keepdims=True)
        idx = jnp.min(jnp.where(work == mx, lane, float(ne)), axis=-1, keepdims=True)
        ids.append(idx)
        vals.append(mx)
        work = jnp.where(lane == idx, -jnp.inf, work)
    es = [jnp.exp(v - vals[0]) for v in vals]
    den = es[0] + es[1] + es[2] + es[3]
    gates = [e / den for e in es]

    member = jnp.zeros((tm, ne), F32)
    for idx in ids:
        member = member + (lane == idx).astype(F32)
    r_i = lax.broadcasted_iota(I32, (tm, tm), 0)
    c_i = lax.broadcasted_iota(I32, (tm, tm), 1)
    tri = (c_i < r_i).astype(BF16)
    before = jnp.dot(tri, member.astype(BF16), preferred_element_type=F32) + cnt_ref[...]
    cnt_ref[...] = cnt_ref[...] + jnp.sum(member, axis=0, keepdims=True)
    ranks = [jnp.sum(jnp.where(lane == idx, before, 0.0), axis=-1, keepdims=True) for idx in ids]

    lane_m = lax.broadcasted_iota(I32, (tm, LANES), 1)
    rec = jnp.zeros((tm, LANES), F32)
    for k, col in enumerate(ids + gates + ranks):
        rec = jnp.where(lane_m == k, col, rec)
    meta_ref[...] = rec.T[0:_META_ROWS, :]
    lane_g = lax.broadcasted_iota(I32, (tm, TOP_K), 1)
    gm = jnp.zeros((tm, TOP_K), F32)
    for k, col in enumerate(gates):
        gm = jnp.where(lane_g == k, col, gm)
    gates_ref[...] = gm


def _post(ym, yd, x2, mod3, woa, wob, g_post, g_pre, w_router, b_router, tiles_per_seq):
    n, d = x2.shape
    tm = TM_POST
    ne = w_router.shape[1]
    tok = lambda w: pl.BlockSpec((tm, w), lambda i: (i, 0))
    full = lambda a: pl.BlockSpec(a.shape, lambda i: (0,) * a.ndim)
    return pl.pallas_call(
        _post_kernel,
        out_shape=[jax.ShapeDtypeStruct((n, d), F32), jax.ShapeDtypeStruct((n, d // LANES, LANES), F32),
                   jax.ShapeDtypeStruct((_META_ROWS, n), F32), jax.ShapeDtypeStruct((n, TOP_K), F32)],
        grid=(n // tm,),
        in_specs=[tok(ym.shape[1]), tok(yd.shape[1]), tok(d),
                  pl.BlockSpec((1, 1, mod3.shape[2]), lambda i: (i // tiles_per_seq, 0, 0)),
                  full(woa), full(wob), full(g_post), full(g_pre), full(w_router), full(b_router)],
        out_specs=[tok(d), pl.BlockSpec((tm, d // LANES, LANES), lambda i: (i, 0, 0)),
                   pl.BlockSpec((_META_ROWS, tm), lambda i: (0, i)), tok(TOP_K)],
        scratch_shapes=[pltpu.VMEM((1, ne), F32)],
        compiler_params=_cparams(("arbitrary",)),
        name="post",
    )(ym, yd, x2, mod3, woa, wob, g_post, g_pre, w_router, b_router)


def _route_kernel(meta_ref, dest_ref, blk_ref):
    ne = N_EXPERTS
    nbl = blk_ref.shape[1]
    e_col = lax.broadcasted_iota(I32, (ne, 1), 0).astype(F32)
    onehots = [(meta_ref[k:k + 1, :] == e_col) for k in range(TOP_K)]
    counts = jnp.zeros((ne, 1), F32)
    for oh in onehots:
        counts = counts + jnp.sum(oh.astype(F32), axis=1, keepdims=True)
    padded = jnp.floor((counts + float(BM_FFN - 1)) / BM_FFN) * BM_FFN
    r_i = lax.broadcasted_iota(I32, (ne, ne), 0)
    c_i = lax.broadcasted_iota(I32, (ne, ne), 1)
    tri = (c_i < r_i).astype(F32)
    padded_f = jnp.broadcast_to(padded, (ne, LANES))
    start = jnp.dot(tri, padded_f, preferred_element_type=F32, precision=lax.Precision.HIGHEST)[:, 0:1]
    for k in range(TOP_K):
        add = jnp.sum(jnp.where(onehots[k], start, 0.0), axis=0, keepdims=True)
        dest_ref[k:k + 1, :] = (meta_ref[2 * TOP_K + k:2 * TOP_K + k + 1, :] + add).astype(I32)
    end = start + padded
    row0 = (lax.broadcasted_iota(I32, (1, nbl), 1) * BM_FFN).astype(F32)
    blk_e = jnp.minimum(jnp.sum((end <= row0).astype(F32), axis=0, keepdims=True), float(ne - 1))
    own = blk_e == e_col
    real_end = jnp.sum(jnp.where(own, start + counts, 0.0), axis=0, keepdims=True)
    nvalid = jnp.clip(real_end - row0, 0.0, float(BM_FFN))
    nact = jnp.sum(padded, axis=0, keepdims=True) / BM_FFN
    rows = lax.broadcasted_iota(I32, blk_ref.shape, 0)
    out = jnp.where(rows == 0, blk_e, jnp.where(rows == 1, nvalid, jnp.broadcast_to(nact, blk_ref.shape)))
    blk_ref[...] = out.astype(I32)


def _route(meta, n_blocks):
    n = meta.shape[1]
    nbl = (n_blocks + LANES - 1) // LANES * LANES
    return pl.pallas_call(
        _route_kernel,
        out_shape=[jax.ShapeDtypeStruct((TOP_K, n), I32), jax.ShapeDtypeStruct((8, nbl), I32)],
        compiler_params=pltpu.CompilerParams(vmem_limit_bytes=VMEM_LIMIT),
        name="route",
    )(meta)


def _dispatch_kernel(dest_ref, nv_ref, na_ref, h_ref, xs_ref, zero_ref, sem, zsem):
    i = pl.program_id(0)
    tm = h_ref.shape[0]
    n = pl.num_programs(0) * tm
    bm = zero_ref.shape[0]

    @pl.when(i == 0)
    def _():
        zero_ref[...] = jnp.zeros_like(zero_ref)

        def fill(b, go):
            copy = pltpu.make_async_copy(zero_ref, xs_ref.at[pl.ds(b * bm, bm)], zsem)

            @pl.when((b >= na_ref[0]) | (nv_ref[b] < bm))
            def _():
                copy.start() if go else copy.wait()

        n_blocks = xs_ref.shape[0] // bm
        lax.fori_loop(0, n_blocks, lambda b, c: (fill(b, True), c)[1], 0)
        lax.fori_loop(0, n_blocks, lambda b, c: (fill(b, False), c)[1], 0)

    def row_copy(t, k):
        dst = dest_ref[k * n + i * tm + t]
        return pltpu.make_async_copy(h_ref.at[t], xs_ref.at[dst], sem)

    for t in range(tm):
        for k in range(TOP_K):
            row_copy(t, k).start()
    for t in range(tm):
        for k in range(TOP_K):
            row_copy(t, k).wait()


def _dispatch(dest_flat, blk_nv, n_act, h2, n_slots):
    n, sub, ln = h2.shape
    tm = TM_ROWS
    return pl.pallas_call(
        _dispatch_kernel,
        out_shape=jax.ShapeDtypeStruct((n_slots, sub, ln), h2.dtype),
        grid_spec=pltpu.PrefetchScalarGridSpec(
            num_scalar_prefetch=3,
            grid=(n // tm,),
            in_specs=[pl.BlockSpec((tm, sub, ln), lambda i, dest, nv, na: (i, 0, 0))],
            out_specs=pl.BlockSpec(memory_space=pl.ANY),
            scratch_shapes=[pltpu.VMEM((BM_FFN, sub, ln), h2.dtype), pltpu.SemaphoreType.DMA,
                            pltpu.SemaphoreType.DMA],
        ),
        compiler_params=_cparams(("arbitrary",)),
        name="dispatch",
    )(dest_flat, blk_nv, n_act, h2)


def _split_w1_kernel(w_ref, g_ref, l_ref, t_ref):
    _, d, cw = w_ref.shape
    for r in range(d // LANES):
        rows = slice(r * LANES, (r + 1) * LANES)
        t_ref[r] = w_ref[0, rows, :].T
        g_ref[0, rows, :] = t_ref[r, pl.ds(0, cw // 2, stride=2), :].T.astype(BF16)
        l_ref[0, rows, :] = t_ref[r, pl.ds(1, cw // 2, stride=2), :].T.astype(BF16)


def _split_w1(w1):
    e, d, f2 = w1.shape
    cw = 1024
    out = jax.ShapeDtypeStruct((e, d, f2 // 2), BF16)
    return pl.pallas_call(
        _split_w1_kernel,
        out_shape=[out, out],
        grid=(e, f2 // cw),
        in_specs=[pl.BlockSpec((1, d, cw), lambda i, j: (i, 0, j))],
        out_specs=[pl.BlockSpec((1, d, cw // 2), lambda i, j: (i, 0, j))] * 2,
        scratch_shapes=[pltpu.VMEM((d // LANES, cw, LANES), F32)],
        compiler_params=_cparams(("parallel", "arbitrary")),
        name="split_w1",
    )(w1)


def _ffn_kernel(be_ref, nv_ref, na_ref, xs_ref, w1g_ref, w1l_ref, w2_ref, b1g_ref, b1l_ref, b2_ref, y_ref,
                w2b_ref):
    b = pl.program_id(0)

    @pl.when((b < na_ref[0]) & ((b == 0) | (be_ref[b] != be_ref[jnp.maximum(b - 1, 0)])))
    def _():
        w2b_ref[...] = w2_ref[0].astype(BF16)

    @pl.when(b < na_ref[0])
    def _():
        bm, sub, ln = xs_ref.shape
        rows = lax.broadcasted_iota(I32, (bm, 1), 0)
        x = jnp.where(rows < nv_ref[b], xs_ref[...].reshape(bm, sub * ln), 0.0).astype(BF16)
        glu = jnp.dot(x, w1g_ref[0], preferred_element_type=F32) + b1g_ref[0]
        lin = jnp.dot(x, w1l_ref[0], preferred_element_type=F32) + b1l_ref[0]
        glu = jnp.minimum(glu, SWIGLU_LIMIT)
        lin = jnp.clip(lin, -SWIGLU_LIMIT, SWIGLU_LIMIT)
        act = glu * jax.nn.sigmoid(SWIGLU_ALPHA * glu) * (lin + 1.0)
        y = jnp.dot(act.astype(BF16), w2b_ref[...], preferred_element_type=F32) + b2_ref[0]
        y_ref[...] = y.reshape(y_ref.shape)

    @pl.when(b >= na_ref[0])
    def _():
        y_ref[...] = jnp.zeros_like(y_ref)


def _ffn(blk_e, blk_nv, n_act, xs, w1g, w1l, w2, b1g, b1l, b2):
    n_slots, sub, ln = xs.shape
    d = sub * ln
    bm = BM_FFN
    f = w1g.shape[2]
    last = lambda b, na: jnp.maximum(jnp.minimum(b, na[0] - 1), 0)
    blk = lambda b, be, nv, na: (last(b, na), 0, 0)
    exp3 = lambda b, be, nv, na: (be[last(b, na)], 0, 0)
    return pl.pallas_call(
        _ffn_kernel,
        out_shape=jax.ShapeDtypeStruct((n_slots, sub, ln), F32),
        grid_spec=pltpu.PrefetchScalarGridSpec(
            num_scalar_prefetch=3,
            grid=(n_slots // bm,),
            in_specs=[pl.BlockSpec((bm, sub, ln), blk),
                      pl.BlockSpec((1, d, f), exp3), pl.BlockSpec((1, d, f), exp3),
                      pl.BlockSpec((1, f, d), exp3),
                      pl.BlockSpec((1, 1, f), exp3), pl.BlockSpec((1, 1, f), exp3),
                      pl.BlockSpec((1, 1, d), exp3)],
            out_specs=pl.BlockSpec((bm, sub, ln), lambda b, be, nv, na: (b, 0, 0)),
            scratch_shapes=[pltpu.VMEM((f, d), BF16)],
        ),
        compiler_params=_cparams(("arbitrary",)),
        name="ffn",
    )(blk_e, blk_nv, n_act, xs, w1g, w1l, w2, b1g, b1l, b2)


def _combine_kernel(dest_ref, y_ref, gates_ref, x1_ref, mod_ref, gpost_ref, o_ref, buf_ref, sem):
    i = pl.program_id(0)
    tm, d = x1_ref.shape
    n = pl.num_programs(0) * tm

    def row_copy(t, k):
        src = dest_ref[k * n + i * tm + t]
        return pltpu.make_async_copy(y_ref.at[src], buf_ref.at[k, t], sem)

    for t in range(tm):
        for k in range(TOP_K):
            row_copy(t, k).start()
    for t in range(tm):
        for k in range(TOP_K):
            row_copy(t, k).wait()
    g = gates_ref[...]
    ffn = g[:, 0:1] * buf_ref[0].reshape(tm, d)
    for k in range(1, TOP_K):
        ffn = ffn + g[:, k:k + 1] * buf_ref[k].reshape(tm, d)
    gt2 = mod_ref[0, :, 5 * d:6 * d]
    o_ref[...] = x1_ref[...] + gt2 * _rms(ffn, gpost_ref[...])


def _combine(dest_flat, y, gates, x1, mod3, g_post, tiles_per_seq):
    n, d = x1.shape
    tm = TM_ROWS
    tok = lambda w: pl.BlockSpec((tm, w), lambda i, dest: (i, 0))
    return pl.pallas_call(
        _combine_kernel,
        out_shape=jax.ShapeDtypeStruct((n, d), F32),
        grid_spec=pltpu.PrefetchScalarGridSpec(
            num_scalar_prefetch=1,
            grid=(n // tm,),
            in_specs=[pl.BlockSpec(memory_space=pl.ANY), tok(TOP_K), tok(d),
                      pl.BlockSpec((1, 1, mod3.shape[2]), lambda i, dest: (i // tiles_per_seq, 0, 0)),
                      pl.BlockSpec(g_post.shape, lambda i, dest: (0, 0))],
            out_specs=tok(d),
            scratch_shapes=[pltpu.VMEM((TOP_K, tm) + y.shape[1:], F32), pltpu.SemaphoreType.DMA],
        ),
        compiler_params=_cparams(("arbitrary",)),
        name="combine",
    )(dest_flat, y, gates, x1, mod3, g_post)


def _layer(x, mod, positions, g_pre_mix, g_post_mix, g_pre_ffn, g_post_ffn, w_in, g_q_a, g_kv_a,
           w_q_b, w_kv_b, w_o, w_router, b_router, w_mlp1, b_mlp1, w_mlp2, b_mlp2):
    b, s, d = x.shape
    n = b * s
    mod3 = mod.reshape(b, 1, mod.shape[1])
    row = lambda g: g.reshape(1, -1)

    win, wq, wkv = _relayout_in_weights(w_in, w_q_b, w_kv_b)
    qm, km, vm, qd, kd, vd, qi, kiw = _inproj(
        x, mod3, positions.reshape(b, s, 1), row(g_pre_mix), win, row(g_q_a), row(g_kv_a), wq, wkv,
        _rope_rows())
    y_mla = _mla(qm, km, vm)
    y_dsa = _dsa(qi, kiw, qd, kd, vd, positions.reshape(b, s, 1),
                 positions.reshape(b, s // QB_DSA, 1, QB_DSA))

    wo = w_o.astype(BF16)
    split = MLA_HEADS * MLA_V
    x1, h2, meta, gates = _post(
        y_mla.reshape(n, -1), y_dsa.reshape(n, -1), x.reshape(n, d), mod3, wo[:split], wo[split:],
        row(g_post_mix), row(g_pre_ffn), w_router, row(b_router), s // TM_POST)

    n_slots = n * TOP_K + N_EXPERTS * BM_FFN
    dest, blk = _route(meta, n_slots // BM_FFN)
    dest_flat = dest.reshape(-1)
    xs = _dispatch(dest_flat, blk[1], blk[2, 0:1], h2, n_slots)
    f = w_mlp2.shape[1]
    w1g, w1l = _split_w1(w_mlp1)
    y = _ffn(blk[0], blk[1], blk[2, 0:1], xs, w1g, w1l, w_mlp2,
             b_mlp1[:, 0::2].reshape(-1, 1, f), b_mlp1[:, 1::2].reshape(-1, 1, f),
             b_mlp2.reshape(-1, 1, d))
    out = _combine(dest_flat, y, gates, x1, mod3, row(g_post_ffn), s // TM_ROWS)
    return out.reshape(b, s, d)


def kernel(x, c, positions, w_ada, b_ada, g_pre_mix, g_post_mix, g_pre_ffn, g_post_ffn, w_in, g_q_a, g_kv_a, w_q_b, w_kv_b, w_o, w_router, b_router, w_mlp1, b_mlp1, w_mlp2, b_mlp2):
    for l in range(w_ada.shape[0]):
        mod = _ada(c, w_ada[l], b_ada[l])
        x = _layer(x, mod, positions, g_pre_mix[l], g_post_mix[l], g_pre_ffn[l], g_post_ffn[l],
                   w_in[l], g_q_a[l], g_kv_a[l], w_q_b[l], w_kv_b[l], w_o[l], w_router[l], b_router[l],
                   w_mlp1[l], b_mlp1[l], w_mlp2[l], b_mlp2[l])
    return x
```

```python
import functools

import jax
import jax.numpy as jnp
from jax import lax
from jax.experimental import pallas as pl
from jax.experimental.pallas import tpu as pltpu

F32 = jnp.float32
BF16 = jnp.bfloat16
I32 = jnp.int32

MLA_HEADS = 8
MLA_Q_RANK = 256
MLA_KV_RANK = 128
MLA_NOPE = 64
MLA_ROPE = 32
MLA_V = 64
ROPE_THETA = 10000.0
DSA_HEADS = 8
DSA_KV_HEADS = 2
DSA_HEAD_DIM = 64
IDX_HEADS = 8
IDX_DIM = 32
IDX_TOPK_MAX = 256
N_EXPERTS = 32
TOP_K = 4
SWIGLU_ALPHA = 1.702
SWIGLU_LIMIT = 7.0
NORM_EPS = 1e-6
NEG_INF = -1e30
LOG2E = 1.4426950408889634

LANES = 128
HEAD_PAD = 128
VMEM_LIMIT = 52 * 1024 * 1024

TM_PROJ = 512
KV_CHUNK = 256
TQ_MLA = 512
QB_DSA = 128
TM_POST = 512
BM_FFN = 256
TM_ROWS = 256

EMPTY_SLOT = -1

_NT = (((1,), (1,)), ((), ()))


def _rms(x, g):
    ms = jnp.mean(x * x, axis=-1, keepdims=True)
    return x * lax.rsqrt(ms + NORM_EPS) * g


def _cparams(sem, vmem=VMEM_LIMIT):
    return pltpu.CompilerParams(dimension_semantics=sem, vmem_limit_bytes=vmem)


def _ada_kernel(c_ref, w_ref, b_ref, o_ref):
    c = c_ref[...]
    cond = c * jax.nn.sigmoid(c)
    o_ref[...] = jnp.dot(cond.astype(BF16), w_ref[...].astype(BF16),
                         preferred_element_type=F32) + b_ref[...]


def _ada(c, w_ada, b_ada):
    b, d = c.shape
    n = w_ada.shape[1]
    tn = 1024
    return pl.pallas_call(
        _ada_kernel,
        out_shape=jax.ShapeDtypeStruct((b, n), F32),
        grid=(n // tn,),
        in_specs=[pl.BlockSpec((b, d), lambda i: (0, 0)),
                  pl.BlockSpec((d, tn), lambda i: (0, i)),
                  pl.BlockSpec((1, tn), lambda i: (0, i))],
        out_specs=pl.BlockSpec((b, tn), lambda i: (0, i)),
        compiler_params=_cparams(("arbitrary",)),
        name="ada",
    )(c, w_ada, b_ada.reshape(1, n))


_O_CQ = 0
_O_CKV = _O_CQ + MLA_Q_RANK
_O_KRA = _O_CKV + MLA_KV_RANK
_O_KRB = _O_KRA + HEAD_PAD
_O_QD = _O_KRB + HEAD_PAD
_O_KD = _O_QD + DSA_HEADS * DSA_HEAD_DIM
_O_VD = _O_KD + DSA_KV_HEADS * DSA_HEAD_DIM
_O_QI = _O_VD + DSA_KV_HEADS * DSA_HEAD_DIM
_O_KIW = _O_QI + IDX_HEADS * IDX_DIM
_W_IN = _O_KIW + LANES


def _inproj_kernel(x_ref, mod_ref, pos_ref, gpre_ref, win_ref, gq_ref, gkv_ref, wq_ref, wkv_ref,
                   rope_ref, qm_ref, km_ref, vm_ref, qd_ref, kd_ref, vd_ref, qi_ref, kiw_ref):
    d = x_ref.shape[2]
    x = x_ref[0]
    sh1 = mod_ref[0, :, 0:d]
    sc1 = mod_ref[0, :, d:2 * d]
    h = _rms(x, gpre_ref[...]) * (1.0 + sc1) + sh1
    proj = jnp.dot(h.astype(BF16), win_ref[...], preferred_element_type=F32)

    ang = pos_ref[0].astype(F32) * rope_ref[0:1, :]
    cos_t = jnp.cos(ang) * rope_ref[1:2, :] + rope_ref[3:4, :]
    sin_t = jnp.sin(ang) * rope_ref[2:3, :]

    hw = MLA_HEADS * HEAD_PAD
    nq = _rms(proj[:, _O_CQ:_O_CQ + MLA_Q_RANK], gq_ref[...])
    qab = jnp.dot(nq.astype(BF16), wq_ref[...], preferred_element_type=F32)
    scale = (MLA_NOPE + MLA_ROPE) ** -0.5 * LOG2E
    nkv = _rms(proj[:, _O_CKV:_O_CKV + MLA_KV_RANK], gkv_ref[...])
    kv = jnp.dot(nkv.astype(BF16), wkv_ref[...], preferred_element_type=F32)
    kr = proj[:, _O_KRA:_O_KRA + HEAD_PAD] * cos_t + proj[:, _O_KRB:_O_KRB + HEAD_PAD] * sin_t
    for hd in range(MLA_HEADS):
        sl = slice(hd * HEAD_PAD, (hd + 1) * HEAD_PAD)
        slb = slice(hw + hd * HEAD_PAD, hw + (hd + 1) * HEAD_PAD)
        qm_ref[0, :, sl] = ((qab[:, sl] * cos_t + qab[:, slb] * sin_t) * scale).astype(BF16)
        km_ref[0, :, sl] = (kv[:, sl] + kr).astype(BF16)

    qd_ref[0] = (proj[:, _O_QD:_O_KD] * (DSA_HEAD_DIM ** -0.5 * LOG2E)).astype(BF16)
    kd_ref[0] = proj[:, _O_KD:_O_VD].astype(BF16)
    qi_ref[0] = (proj[:, _O_QI:_O_KIW] * (IDX_DIM ** -0.5)).astype(BF16)
    kiw_ref[0] = proj[:, _O_KIW:_W_IN]
    ck = vm_ref.shape[3]
    for t in range(x_ref.shape[1] // ck):
        rows = slice(t * ck, (t + 1) * ck)
        vm_ref[0, t] = kv[rows, hw:hw + MLA_HEADS * MLA_V].T.astype(BF16)
        vd_ref[0, t] = proj[rows, _O_VD:_O_QI].T.astype(BF16)


def _relayout_in_weights(w_in, w_q_b, w_kv_b):
    d = w_in.shape[0]
    half = MLA_ROPE // 2
    o = 0
    segs = {}
    for name, width in (("cq", MLA_Q_RANK), ("ckv", MLA_KV_RANK), ("kr", MLA_ROPE),
                        ("qd", DSA_HEADS * DSA_HEAD_DIM), ("kd", DSA_KV_HEADS * DSA_HEAD_DIM),
                        ("vd", DSA_KV_HEADS * DSA_HEAD_DIM), ("qi", IDX_HEADS * IDX_DIM),
                        ("ki", IDX_DIM), ("wi", IDX_HEADS)):
        segs[name] = w_in[:, o:o + width]
        o += width
    z = lambda n: jnp.zeros((d, n), w_in.dtype)
    x1, x2 = segs["kr"][:, :half], segs["kr"][:, half:]
    tail = HEAD_PAD - MLA_NOPE - MLA_ROPE
    kra = jnp.concatenate([z(MLA_NOPE), x1, x2, z(tail)], axis=1)
    krb = jnp.concatenate([z(MLA_NOPE), x2, x1, z(tail)], axis=1)
    win = jnp.concatenate([segs["cq"], segs["ckv"], kra, krb, segs["qd"], segs["kd"], segs["vd"],
                           segs["qi"], segs["ki"], segs["wi"], z(LANES - IDX_DIM - IDX_HEADS)], axis=1)
    r = w_q_b.shape[0]
    wq = w_q_b.reshape(r, MLA_HEADS, MLA_NOPE + MLA_ROPE)
    zq = lambda n: jnp.zeros((r, MLA_HEADS, n), w_q_b.dtype)
    wqa = jnp.concatenate([wq, zq(tail)], axis=2).reshape(r, MLA_HEADS * HEAD_PAD)
    wqb = jnp.concatenate([zq(MLA_NOPE), wq[:, :, MLA_NOPE + half:], wq[:, :, MLA_NOPE:MLA_NOPE + half],
                           zq(tail)], axis=2).reshape(r, MLA_HEADS * HEAD_PAD)
    rk = w_kv_b.shape[0]
    wkv = w_kv_b.reshape(rk, MLA_HEADS, MLA_NOPE + MLA_V)
    wkn = jnp.concatenate([wkv[:, :, :MLA_NOPE], jnp.zeros((rk, MLA_HEADS, HEAD_PAD - MLA_NOPE), w_kv_b.dtype)],
                          axis=2).reshape(rk, MLA_HEADS * HEAD_PAD)
    wv = wkv[:, :, MLA_NOPE:].reshape(rk, MLA_HEADS * MLA_V)
    return (win.astype(BF16), jnp.concatenate([wqa, wqb], axis=1).astype(BF16),
            jnp.concatenate([wkn, wv], axis=1).astype(BF16))


def _rope_rows():
    half = MLA_ROPE // 2
    lane = jnp.arange(LANES)
    freqs = ROPE_THETA ** (-jnp.arange(half, dtype=F32) / half)
    in_x1 = (lane >= MLA_NOPE) & (lane < MLA_NOPE + half)
    in_x2 = (lane >= MLA_NOPE + half) & (lane < MLA_NOPE + MLA_ROPE)
    fr = jnp.where(in_x1 | in_x2, freqs[(lane - MLA_NOPE) % half], 0.0)
    cosm = (in_x1 | in_x2).astype(F32)
    sinm = jnp.where(in_x1, -1.0, jnp.where(in_x2, 1.0, 0.0))
    nopem = (lane < MLA_NOPE).astype(F32)
    rows = jnp.stack([fr, cosm, sinm, nopem], axis=0).astype(F32)
    return jnp.concatenate([rows, jnp.zeros((4, LANES), F32)], axis=0)


def _inproj(x, mod3, pos3, g_pre, win, g_q, g_kv, wq, wkv, rope_rows):
    b, s, d = x.shape
    tm = TM_PROJ
    ck = KV_CHUNK
    tok = lambda w: pl.BlockSpec((1, tm, w), lambda bi, i: (bi, i, 0))
    full = lambda a: pl.BlockSpec(a.shape, lambda bi, i: (0,) * a.ndim)
    tr = lambda w: pl.BlockSpec((1, tm // ck, w, ck), lambda bi, i: (bi, i, 0, 0))
    tok_out = lambda w, dt: (jax.ShapeDtypeStruct((b, s, w), dt), tok(w))
    tr_out = lambda w: (jax.ShapeDtypeStruct((b, s // ck, w, ck), BF16), tr(w))
    outs = [tok_out(MLA_HEADS * HEAD_PAD, BF16), tok_out(MLA_HEADS * HEAD_PAD, BF16),
            tr_out(MLA_HEADS * MLA_V), tok_out(DSA_HEADS * DSA_HEAD_DIM, BF16),
            tok_out(DSA_KV_HEADS * DSA_HEAD_DIM, BF16), tr_out(DSA_KV_HEADS * DSA_HEAD_DIM),
            tok_out(IDX_HEADS * IDX_DIM, BF16), tok_out(LANES, F32)]
    return pl.pallas_call(
        _inproj_kernel,
        out_shape=[o[0] for o in outs],
        grid=(b, s // tm),
        in_specs=[tok(d),
                  pl.BlockSpec((1, 1, mod3.shape[2]), lambda bi, i: (bi, 0, 0)),
                  tok(1), full(g_pre), full(win), full(g_q), full(g_kv), full(wq), full(wkv),
                  full(rope_rows)],
        out_specs=[o[1] for o in outs],
        compiler_params=_cparams(("parallel", "arbitrary")),
        name="inproj",
    )(x, mod3, pos3, g_pre, win, g_q, g_kv, wq, wkv, rope_rows)


def _mla_kernel(q_ref, k_ref, vt_ref, o_ref, m_ref, l_ref, acc_ref):
    j = pl.program_id(1)
    tq = q_ref.shape[1]
    tk = k_ref.shape[1] // vt_ref.shape[1]
    m_ref[...] = jnp.full_like(m_ref, NEG_INF)
    l_ref[...] = jnp.zeros_like(l_ref)
    acc_ref[...] = jnp.zeros_like(acc_ref)
    per_q = tq // tk
    krow = lax.broadcasted_iota(I32, (tk, 1), 0)
    qcol = j * tq + lax.broadcasted_iota(I32, (1, tq), 1)

    def chunk(c, diagonal):
        off = pl.multiple_of(c * tk, tk)
        m_all = m_ref[...]
        l_all = l_ref[...]
        accs = [acc_ref[hd] for hd in range(MLA_HEADS)]
        ms, ls = [], []

        def scores(hd):
            lanes = slice(hd * HEAD_PAD, (hd + 1) * HEAD_PAD)
            return lax.dot_general(k_ref[0, pl.ds(off, tk), lanes], q_ref[0, :, lanes], _NT,
                                   preferred_element_type=F32)

        s_next = scores(0)
        pend = None
        for hd in range(MLA_HEADS):
            s = s_next
            if hd + 1 < MLA_HEADS:
                s_next = scores(hd + 1)
            if diagonal:
                s = jnp.where(off + krow <= qcol, s, NEG_INF)
            m_old = m_all[hd:hd + 1, :]
            m_new = jnp.maximum(m_old, jnp.max(s, axis=0, keepdims=True))
            alpha = jnp.exp2(m_old - m_new)
            p = jnp.exp2(s - m_new)
            ls.append(alpha * l_all[hd:hd + 1, :] + jnp.sum(p, axis=0, keepdims=True))
            ms.append(m_new)
            pv = jnp.dot(vt_ref[0, c, hd * MLA_V:(hd + 1) * MLA_V, :], p.astype(BF16),
                         preferred_element_type=F32)
            if pend is not None:
                accs[pend[0]] = pend[1] * accs[pend[0]] + pend[2]
            pend = (hd, alpha, pv)
        accs[pend[0]] = pend[1] * accs[pend[0]] + pend[2]
        m_ref[...] = jnp.concatenate(ms, axis=0)
        l_ref[...] = jnp.concatenate(ls, axis=0)
        for hd in range(MLA_HEADS):
            acc_ref[hd] = accs[hd]

    def body(c, carry):
        chunk(c, False)
        return carry

    lax.fori_loop(0, j * per_q, body, 0)
    for t in range(per_q):
        chunk(j * per_q + t, True)
    o_t = jnp.concatenate([acc_ref[hd] / l_ref[hd:hd + 1, :] for hd in range(MLA_HEADS)], axis=0)
    o_ref[0] = o_t.T.astype(BF16)


def _mla(qm, km, vmt):
    b, s, hw = qm.shape
    tq = TQ_MLA
    _, nck, vw, ck = vmt.shape
    return pl.pallas_call(
        _mla_kernel,
        out_shape=jax.ShapeDtypeStruct((b, s, vw), BF16),
        grid=(b, s // tq),
        in_specs=[pl.BlockSpec((1, tq, hw), lambda bi, j: (bi, j, 0)),
                  pl.BlockSpec((1, s, hw), lambda bi, j: (bi, 0, 0)),
                  pl.BlockSpec((1, nck, vw, ck), lambda bi, j: (bi, 0, 0, 0))],
        out_specs=pl.BlockSpec((1, tq, vw), lambda bi, j: (bi, j, 0)),
        scratch_shapes=[pltpu.VMEM((MLA_HEADS, tq), F32), pltpu.VMEM((MLA_HEADS, tq), F32),
                        pltpu.VMEM((MLA_HEADS, MLA_V, tq), F32)],
        compiler_params=_cparams(("parallel", "arbitrary")),
        name="mla",
    )(qm, km, vmt)


_KEY_NEG_INF = -2139095041
_KEY_POS_INF = 2139095040
_I32_MAX = 2147483647


def _key_to_f32(k):
    bits = k ^ ((k >> 31) & _I32_MAX)
    return lax.bitcast_convert_type(bits, F32)


def _dsa_kernel(qi_ref, kiw_ref, qd_ref, kd_ref, vdt_ref, posk_ref, posq_ref, y_ref, sc_ref, acc_ref):
    j = pl.program_id(1)
    qb, ck = QB_DSA, KV_CHUNK
    nch = (j * qb + qb + ck - 1) // ck
    n_sel = IDX_TOPK_MAX

    kiw_q = kiw_ref[0, pl.ds(pl.multiple_of(j * qb, qb), qb), :]
    w_t = kiw_q.T[IDX_DIM:IDX_DIM + IDX_HEADS, :] * (IDX_HEADS ** -0.5)
    qi = qi_ref[0]
    qi_stack = jnp.concatenate([qi[:, hd * IDX_DIM:(hd + 1) * IDX_DIM] for hd in range(IDX_HEADS)], axis=0)
    q_idx = j * qb + lax.broadcasted_iota(I32, (1, qb), 1)
    k_iota = lax.broadcasted_iota(I32, (ck, 1), 0)

    def idx_chunk(c, carry):
        off = pl.multiple_of(c * ck, ck)
        ki = kiw_ref[0, pl.ds(off, ck), :][:, 0:IDX_DIM].astype(BF16)
        r = lax.dot_general(ki, qi_stack, _NT, preferred_element_type=F32)
        acc = jnp.zeros((ck, qb), F32)
        for hd in range(IDX_HEADS):
            acc = acc + w_t[hd:hd + 1, :] * jnp.maximum(r[:, hd * qb:(hd + 1) * qb], 0.0)
        sc_ref[pl.ds(off, ck), :] = jnp.where(off + k_iota <= q_idx, acc, NEG_INF)
        return carry

    lax.fori_loop(0, nch, idx_chunk, 0)

    n_part = 4

    def count(pred_fn):
        def body(c, parts):
            off = pl.multiple_of(c * ck, ck)
            hit = pred_fn(sc_ref[pl.ds(off, ck), :], off).astype(F32)
            rows = ck // n_part
            return tuple(p + jnp.sum(hit[i * rows:(i + 1) * rows].reshape(rows // 8, 8, qb), axis=0)
                         for i, p in enumerate(parts))
        parts = lax.fori_loop(0, nch, body, (jnp.zeros((8, qb), F32),) * n_part)
        return jnp.sum((parts[0] + parts[1]) + (parts[2] + parts[3]), axis=0, keepdims=True)

    def no_tie(_):
        return jnp.full((1, qb), jnp.inf, F32), jnp.full((1, qb), _I32_MAX, I32)

    def select(_):
        def bisect(_, st):
            lo, hi, c_lo = st
            mid = (lo & hi) + ((lo ^ hi) >> 1)
            t = _key_to_f32(mid)
            cnt = count(lambda x, off: x >= t)
            ok = cnt >= n_sel
            return jnp.where(ok, mid, lo), jnp.where(ok, hi, mid), jnp.where(ok, cnt, c_lo)

        per_round = 4

        def round_(st):
            it, lo, hi, c_lo = st
            lo, hi, c_lo = lax.fori_loop(0, per_round, bisect, (lo, hi, c_lo))
            return it + per_round, lo, hi, c_lo

        def unsettled(st):
            it, _, _, c_lo = st
            return (it < 32) & (jnp.max(jnp.abs(c_lo - n_sel)) > 0.5)

        st0 = (jnp.int32(0), jnp.full((1, qb), _KEY_NEG_INF, I32), jnp.full((1, qb), _KEY_POS_INF, I32),
               jnp.broadcast_to((nch * ck).astype(F32), (1, qb)))
        _, lo, _, c_lo = lax.while_loop(unsettled, round_, st0)
        t_lo = _key_to_f32(lo)

        def tie_search(_):
            t_hi = _key_to_f32(lo + 1)
            need = n_sel - count(lambda x, off: x >= t_hi)

            def step(_, lohi):
                mlo, mhi = lohi
                mid = (mlo + mhi) >> 1
                cnt = count(lambda x, off: (x >= t_lo) & jnp.logical_not(x >= t_hi)
                            & (off + k_iota <= mid))
                ok = cnt >= need
                return jnp.where(ok, mlo, mid), jnp.where(ok, mid, mhi)
            mlo0 = jnp.full((1, qb), -1, I32)
            mhi0 = jnp.full((1, qb), sc_ref.shape[0] - 1, I32)
            return t_hi, lax.fori_loop(0, 12, step, (mlo0, mhi0))[1]

        tied = jnp.max(jnp.abs(c_lo - n_sel)) > 0.5
        t_hi, m_sel = lax.cond(tied, tie_search, no_tie, 0)
        return t_lo, t_hi, m_sel

    def all_causal(_):
        return (jnp.full((1, qb), 0.1 * NEG_INF, F32), jnp.full((1, qb), jnp.inf, F32),
                jnp.full((1, qb), _I32_MAX, I32))

    t_lo, t_hi, m_sel = lax.cond((j + 1) * qb > n_sel, select, all_causal, 0)

    grp = DSA_HEADS // DSA_KV_HEADS
    dh = DSA_HEAD_DIM
    qd = qd_ref[0]
    q_groups = [jnp.concatenate([qd[:, (g * grp + i) * dh:(g * grp + i + 1) * dh] for i in range(grp)], axis=0)
                for g in range(DSA_KV_HEADS)]
    posq = posq_ref[0, j]
    acc_ref[...] = jnp.zeros_like(acc_ref)

    def att_chunk(c, carry):
        ms, ls = carry
        off = pl.multiple_of(c * ck, ck)
        x = sc_ref[pl.ds(off, ck), :]
        sel = (x >= t_hi) | ((x >= t_lo) & (off + k_iota <= m_sel))
        dist = jnp.abs(posk_ref[0, pl.ds(off, ck), :] - posq).astype(F32)
        kd = kd_ref[0, pl.ds(off, ck), :]
        new_ms, new_ls = [], []
        s_all = [lax.dot_general(kd[:, g * dh:(g + 1) * dh], q_groups[g], _NT,
                                 preferred_element_type=F32) for g in range(DSA_KV_HEADS)]
        for g in range(DSA_KV_HEADS):
            s = s_all[g]
            ps, alphas = [], []
            for i in range(grp):
                hd = g * grp + i
                slope = 2.0 ** (-8.0 * (hd + 1) / DSA_HEADS) * LOG2E
                si = jnp.where(sel, s[:, i * qb:(i + 1) * qb] - slope * dist, NEG_INF)
                m_new = jnp.maximum(ms[hd], jnp.max(si, axis=0, keepdims=True))
                alpha = jnp.exp2(ms[hd] - m_new)
                p = jnp.exp2(si - m_new)
                new_ls.append(alpha * ls[hd] + jnp.sum(p, axis=0, keepdims=True))
                new_ms.append(m_new)
                ps.append(p.astype(BF16))
                alphas.append(alpha)
            pv = jnp.dot(vdt_ref[0, c, g * dh:(g + 1) * dh, :], jnp.concatenate(ps, axis=1),
                         preferred_element_type=F32)
            acc_ref[g] = jnp.concatenate(alphas, axis=1) * acc_ref[g] + pv
        return tuple(new_ms), tuple(new_ls)

    init = ((jnp.full((1, qb), NEG_INF, F32),) * DSA_HEADS, (jnp.zeros((1, qb), F32),) * DSA_HEADS)
    _, ls = lax.fori_loop(0, nch, att_chunk, init)
    o_t = jnp.concatenate([acc_ref[hd // grp][:, (hd % grp) * qb:(hd % grp + 1) * qb] / ls[hd]
                           for hd in range(DSA_HEADS)], axis=0)
    y_ref[0] = o_t.T.astype(BF16)


def _dsa(qi, kiw, qd, kd, vdt, posk, posq):
    b, s, _ = qd.shape
    qb = QB_DSA
    _, nck, vw, ck = vdt.shape
    grp = DSA_HEADS // DSA_KV_HEADS
    blk = lambda w: pl.BlockSpec((1, qb, w), lambda bi, j: (bi, j, 0))
    seq = lambda w: pl.BlockSpec((1, s, w), lambda bi, j: (bi, 0, 0))
    return pl.pallas_call(
        _dsa_kernel,
        out_shape=jax.ShapeDtypeStruct((b, s, qd.shape[2]), BF16),
        grid=(b, s // qb),
        in_specs=[blk(qi.shape[2]), seq(kiw.shape[2]), blk(qd.shape[2]), seq(kd.shape[2]),
                  pl.BlockSpec((1, nck, vw, ck), lambda bi, j: (bi, 0, 0, 0)),
                  seq(1),
                  pl.BlockSpec((1, s // qb, 1, qb), lambda bi, j: (bi, 0, 0, 0))],
        out_specs=blk(qd.shape[2]),
        scratch_shapes=[pltpu.VMEM((s, qb), F32),
                        pltpu.VMEM((DSA_KV_HEADS, DSA_HEAD_DIM, grp * qb), F32)],
        compiler_params=_cparams(("parallel", "arbitrary")),
        name="dsa",
    )(qi, kiw, qd, kd, vdt, posk, posq)


_META_ROWS = 16


def _post_kernel(ym_ref, yd_ref, x_ref, mod_ref, woa_ref, wob_ref, gpost_ref, gpre_ref, wr_ref, br_ref,
                 x1_ref, h2_ref, meta_ref, gates_ref, cnt_ref):
    i = pl.program_id(0)
    d = x_ref.shape[1]
    tm = x_ref.shape[0]

    @pl.when(i == 0)
    def _():
        cnt_ref[...] = jnp.zeros_like(cnt_ref)

    mix = (jnp.dot(ym_ref[...], woa_ref[...], preferred_element_type=F32)
           + jnp.dot(yd_ref[...], wob_ref[...], preferred_element_type=F32))
    gt1 = mod_ref[0, :, 2 * d:3 * d]
    sh2 = mod_ref[0, :, 3 * d:4 * d]
    sc2 = mod_ref[0, :, 4 * d:5 * d]
    x1 = x_ref[...] + gt1 * _rms(mix, gpost_ref[...])
    x1_ref[...] = x1
    h2 = _rms(x1, gpre_ref[...]) * (1.0 + sc2) + sh2
    h2_ref[...] = h2.reshape(h2_ref.shape)

    logits = jnp.dot(h2, wr_ref[...], preferred_element_type=F32,
                     precision=lax.Precision.HIGHEST) + br_ref[...]
    ne = logits.shape[1]
    lane = lax.broadcasted_iota(I32, (tm, ne), 1).astype(F32)
    work = logits
    ids, vals = [], []
    for _ in range(TOP_K):
        mx = jnp.max(work, axis=-1, keepdims=True)
        idx = jnp.min(jnp.where(work == mx, lane, float(ne)), axis=-1, keepdims=True)
        ids.append(idx)
        vals.append(mx)
        work = jnp.where(lane == idx, -jnp.inf, work)
    es = [jnp.exp(v - vals[0]) for v in vals]
    den = es[0] + es[1] + es[2] + es[3]
    gates = [e / den for e in es]

    member = jnp.zeros((tm, ne), F32)
    for idx in ids:
        member = member + (lane == idx).astype(F32)
    r_i = lax.broadcasted_iota(I32, (tm, tm), 0)
    c_i = lax.broadcasted_iota(I32, (tm, tm), 1)
    tri = (c_i < r_i).astype(BF16)
    before = jnp.dot(tri, member.astype(BF16), preferred_element_type=F32) + cnt_ref[...]
    cnt_ref[...] = cnt_ref[...] + jnp.sum(member, axis=0, keepdims=True)
    ranks = [jnp.sum(jnp.where(lane == idx, before, 0.0), axis=-1, keepdims=True) for idx in ids]

    lane_m = lax.broadcasted_iota(I32, (tm, LANES), 1)
    rec = jnp.zeros((tm, LANES), F32)
    for k, col in enumerate(ids + gates + ranks):
        rec = jnp.where(lane_m == k, col, rec)
    meta_ref[...] = rec.T[0:_META_ROWS, :]
    lane_g = lax.broadcasted_iota(I32, (tm, TOP_K), 1)
    gm = jnp.zeros((tm, TOP_K), F32)
    for k, col in enumerate(gates):
        gm = jnp.where(lane_g == k, col, gm)
    gates_ref[...] = gm


def _post(ym, yd, x2, mod3, woa, wob, g_post, g_pre, w_router, b_router, tiles_per_seq):
    n, d = x2.shape
    tm = TM_POST
    ne = w_router.shape[1]
    tok = lambda w: pl.BlockSpec((tm, w), lambda i: (i, 0))
    full = lambda a: pl.BlockSpec(a.shape, lambda i: (0,) * a.ndim)
    return pl.pallas_call(
        _post_kernel,
        out_shape=[jax.ShapeDtypeStruct((n, d), F32), jax.ShapeDtypeStruct((n, d // LANES, LANES), F32),
                   jax.ShapeDtypeStruct((_META_ROWS, n), F32), jax.ShapeDtypeStruct((n, TOP_K), F32)],
        grid=(n // tm,),
        in_specs=[tok(ym.shape[1]), tok(yd.shape[1]), tok(d),
                  pl.BlockSpec((1, 1, mod3.shape[2]), lambda i: (i // tiles_per_seq, 0, 0)),
                  full(woa), full(wob), full(g_post), full(g_pre), full(w_router), full(b_router)],
        out_specs=[tok(d), pl.BlockSpec((tm, d // LANES, LANES), lambda i: (i, 0, 0)),
                   pl.BlockSpec((_META_ROWS, tm), lambda i: (0, i)), tok(TOP_K)],
        scratch_shapes=[pltpu.VMEM((1, ne), F32)],
        compiler_params=_cparams(("arbitrary",)),
        name="post",
    )(ym, yd, x2, mod3, woa, wob, g_post, g_pre, w_router, b_router)


def _route_kernel(meta_ref, dest_ref, blk_ref):
    ne = N_EXPERTS
    nbl = blk_ref.shape[1]
    e_col = lax.broadcasted_iota(I32, (ne, 1), 0).astype(F32)
    onehots = [(meta_ref[k:k + 1, :] == e_col) for k in range(TOP_K)]
    counts = jnp.zeros((ne, 1), F32)
    for oh in onehots:
        counts = counts + jnp.sum(oh.astype(F32), axis=1, keepdims=True)
    padded = jnp.floor((counts + float(BM_FFN - 1)) / BM_FFN) * BM_FFN
    r_i = lax.broadcasted_iota(I32, (ne, ne), 0)
    c_i = lax.broadcasted_iota(I32, (ne, ne), 1)
    tri = (c_i < r_i).astype(F32)
    padded_f = jnp.broadcast_to(padded, (ne, LANES))
    start = jnp.dot(tri, padded_f, preferred_element_type=F32, precision=lax.Precision.HIGHEST)[:, 0:1]
    for k in range(TOP_K):
        add = jnp.sum(jnp.where(onehots[k], start, 0.0), axis=0, keepdims=True)
        dest_ref[k:k + 1, :] = (meta_ref[2 * TOP_K + k:2 * TOP_K + k + 1, :] + add).astype(I32)
    end = start + padded
    row0 = (lax.broadcasted_iota(I32, (1, nbl), 1) * BM_FFN).astype(F32)
    blk_e = jnp.minimum(jnp.sum((end <= row0).astype(F32), axis=0, keepdims=True), float(ne - 1))
    own = blk_e == e_col
    real_end = jnp.sum(jnp.where(own, start + counts, 0.0), axis=0, keepdims=True)
    nvalid = jnp.clip(real_end - row0, 0.0, float(BM_FFN))
    nact = jnp.sum(padded, axis=0, keepdims=True) / BM_FFN
    rows = lax.broadcasted_iota(I32, blk_ref.shape, 0)
    out = jnp.where(rows == 0, blk_e, jnp.where(rows == 1, nvalid, jnp.broadcast_to(nact, blk_ref.shape)))
    blk_ref[...] = out.astype(I32)


def _route(meta, n_blocks):
    n = meta.shape[1]
    nbl = (n_blocks + LANES - 1) // LANES * LANES
    return pl.pallas_call(
        _route_kernel,
        out_shape=[jax.ShapeDtypeStruct((TOP_K, n), I32), jax.ShapeDtypeStruct((8, nbl), I32)],
        compiler_params=pltpu.CompilerParams(vmem_limit_bytes=VMEM_LIMIT),
        name="route",
    )(meta)


def _dispatch_kernel(dest_ref, nv_ref, na_ref, h_ref, w_ref, xs_ref, code_ref, g_ref, l_ref, zero_ref, t_ref,
                     sem, zsem):
    i = pl.program_id(0)
    tm = h_ref.shape[0]
    n = pl.num_programs(0) * tm
    bm = zero_ref.shape[0]

    @pl.when(i == 0)
    def _():
        zero_ref[...] = jnp.zeros_like(zero_ref)

        def fill(b, go):
            copy = pltpu.make_async_copy(zero_ref, xs_ref.at[pl.ds(b * bm, bm)], zsem)

            @pl.when((b >= na_ref[0]) | (nv_ref[b] < bm))
            def _():
                copy.start() if go else copy.wait()

        def mark_empty(b, c):
            first = jnp.where(b >= na_ref[0], 0, nv_ref[b])

            def mark(r, c2):
                code_ref[b * bm + r] = EMPTY_SLOT
                return c2
            return lax.fori_loop(first, bm, mark, c)

        n_blocks = xs_ref.shape[0] // bm
        lax.fori_loop(0, n_blocks, lambda b, c: (fill(b, True), c)[1], 0)
        lax.fori_loop(0, n_blocks, mark_empty, 0)
        lax.fori_loop(0, n_blocks, lambda b, c: (fill(b, False), c)[1], 0)

    def row_copy(t, k):
        dst = dest_ref[k * n + i * tm + t]
        return pltpu.make_async_copy(h_ref.at[t], xs_ref.at[dst], sem)

    for t in range(tm):
        for k in range(TOP_K):
            code_ref[dest_ref[k * n + i * tm + t]] = k * n + i * tm + t
            row_copy(t, k).start()
    _split_w1_block(w_ref, g_ref, l_ref, t_ref)
    for t in range(tm):
        for k in range(TOP_K):
            row_copy(t, k).wait()


def _split_w1_block(w_ref, g_ref, l_ref, t_ref):
    _, d, cw = w_ref.shape
    for r in range(d // LANES):
        rows = slice(r * LANES, (r + 1) * LANES)
        t_ref[r] = w_ref[0, rows, :].T
        g_ref[0, rows, :] = t_ref[r, pl.ds(0, cw // 2, stride=2), :].T.astype(BF16)
        l_ref[0, rows, :] = t_ref[r, pl.ds(1, cw // 2, stride=2), :].T.astype(BF16)


def _dispatch(dest_flat, blk_nv, n_act, h2, w1, n_slots):
    n, sub, ln = h2.shape
    e, d, f2 = w1.shape
    tm = TM_ROWS
    steps = n // tm
    per_e = steps // e
    cw = f2 // per_e
    assert per_e * e == steps and cw * per_e == f2 and cw % (2 * LANES) == 0
    w_out = jax.ShapeDtypeStruct((e, d, f2 // 2), BF16)
    slab = lambda width: pl.BlockSpec((1, d, width), lambda i, dest, nv, na: (i // per_e, 0, i % per_e))
    return pl.pallas_call(
        _dispatch_kernel,
        out_shape=[jax.ShapeDtypeStruct((n_slots, sub, ln), h2.dtype),
                   jax.ShapeDtypeStruct((n_slots,), I32), w_out, w_out],
        grid_spec=pltpu.PrefetchScalarGridSpec(
            num_scalar_prefetch=3,
            grid=(steps,),
            in_specs=[pl.BlockSpec((tm, sub, ln), lambda i, dest, nv, na: (i, 0, 0)), slab(cw)],
            out_specs=[pl.BlockSpec(memory_space=pl.ANY), pl.BlockSpec(memory_space=pltpu.SMEM),
                       slab(cw // 2), slab(cw // 2)],
            scratch_shapes=[pltpu.VMEM((BM_FFN, sub, ln), h2.dtype),
                            pltpu.VMEM((d // LANES, cw, LANES), F32),
                            pltpu.SemaphoreType.DMA, pltpu.SemaphoreType.DMA],
        ),
        compiler_params=_cparams(("arbitrary",)),
        name="dispatch",
    )(dest_flat, blk_nv, n_act, h2, w1)


def _ffn_kernel(be_ref, nv_ref, na_ref, code_ref, xs_ref, w1g_ref, w1l_ref, w2_ref, b1g_ref, b1l_ref, b2_ref,
                yt_ref, w2b_ref, stage_ref, sem):
    b = pl.program_id(0)
    steps = pl.num_programs(0)
    bm, sub, ln = xs_ref.shape
    n_rows = yt_ref.shape[0] - 2 * bm
    active = b < na_ref[0]

    def row_copy(step, slot, r, valid):
        code = jnp.where(valid, code_ref[step * bm + r], EMPTY_SLOT)
        dst = jnp.where(code == EMPTY_SLOT, n_rows + slot * bm + r, code)
        return pltpu.make_async_copy(stage_ref.at[slot, r], yt_ref.at[dst], sem.at[slot])

    def scatter(step, slot, go, valid=True):
        for r in range(bm):
            row_copy(step, slot, r, valid).start() if go else row_copy(step, slot, r, valid).wait()

    @pl.when(b == 0)
    def _():
        stage_ref[...] = jnp.zeros_like(stage_ref)
        spare = [pltpu.make_async_copy(stage_ref.at[p], yt_ref.at[pl.ds(n_rows + p * bm, bm)], sem.at[p])
                 for p in range(2)]
        for copy in spare:
            copy.start()
        for copy in spare:
            copy.wait()

    @pl.when(active & ((b == 0) | (be_ref[b] != be_ref[jnp.maximum(b - 1, 0)])))
    def _():
        w2b_ref[...] = w2_ref[0].astype(BF16)

    @pl.when(active)
    def _():
        prev = jnp.maximum(b - 1, 0)
        other = (b + 1) % 2
        scatter(prev, other, True, valid=b >= 1)
        rows = lax.broadcasted_iota(I32, (bm, 1), 0)
        x = jnp.where(rows < nv_ref[b], xs_ref[...].reshape(bm, sub * ln), 0.0).astype(BF16)
        glu = jnp.dot(x, w1g_ref[0], preferred_element_type=F32) + b1g_ref[0]
        lin = jnp.dot(x, w1l_ref[0], preferred_element_type=F32) + b1l_ref[0]
        glu = jnp.minimum(glu, SWIGLU_LIMIT)
        lin = jnp.clip(lin, -SWIGLU_LIMIT, SWIGLU_LIMIT)
        act = glu * jax.nn.sigmoid(SWIGLU_ALPHA * glu) * (lin + 1.0)
        y = jnp.dot(act.astype(BF16), w2b_ref[...], preferred_element_type=F32) + b2_ref[0]
        stage_ref[b % 2] = y.reshape(bm, sub, ln)
        scatter(prev, other, False, valid=b >= 1)

    @pl.when((b >= 1) & (b == na_ref[0]))
    def _():
        scatter(b - 1, (b - 1) % 2, True)
        scatter(b - 1, (b - 1) % 2, False)

    @pl.when(active & (b == steps - 1))
    def _():
        scatter(b, b % 2, True)
        scatter(b, b % 2, False)


def _ffn(blk_e, blk_nv, n_act, code, xs, w1g, w1l, w2, b1g, b1l, b2, n_rows):
    n_slots, sub, ln = xs.shape
    d = sub * ln
    bm = BM_FFN
    f = w1g.shape[2]
    last = lambda b, na: jnp.maximum(jnp.minimum(b, na[0] - 1), 0)
    blk = lambda b, be, nv, na, code: (last(b, na), 0, 0)
    exp3 = lambda b, be, nv, na, code: (be[last(b, na)], 0, 0)
    return pl.pallas_call(
        _ffn_kernel,
        out_shape=jax.ShapeDtypeStruct((n_rows + 2 * bm, sub, ln), F32),
        grid_spec=pltpu.PrefetchScalarGridSpec(
            num_scalar_prefetch=4,
            grid=(n_slots // bm,),
            in_specs=[pl.BlockSpec((bm, sub, ln), blk),
                      pl.BlockSpec((1, d, f), exp3), pl.BlockSpec((1, d, f), exp3),
                      pl.BlockSpec((1, f, d), exp3),
                      pl.BlockSpec((1, 1, f), exp3), pl.BlockSpec((1, 1, f), exp3),
                      pl.BlockSpec((1, 1, d), exp3)],
            out_specs=pl.BlockSpec(memory_space=pl.ANY),
            scratch_shapes=[pltpu.VMEM((f, d), BF16), pltpu.VMEM((2, bm, sub, ln), F32),
                            pltpu.SemaphoreType.DMA((2,))],
        ),
        compiler_params=_cparams(("arbitrary",)),
        name="ffn",
    )(blk_e, blk_nv, n_act, code, xs, w1g, w1l, w2, b1g, b1l, b2)


def _combine_kernel(*refs):
    y_refs, (gates_ref, x1_ref, mod_ref, gpost_ref, o_ref) = refs[:TOP_K], refs[TOP_K:]
    tm, d = x1_ref.shape
    g = gates_ref[...]
    ffn = g[:, 0:1] * y_refs[0][...].reshape(tm, d)
    for k in range(1, TOP_K):
        ffn = ffn + g[:, k:k + 1] * y_refs[k][...].reshape(tm, d)
    gt2 = mod_ref[0, :, 5 * d:6 * d]
    o_ref[...] = x1_ref[...] + gt2 * _rms(ffn, gpost_ref[...])


def _combine(yt, gates, x1, mod3, g_post, tiles_per_seq):
    n, d = x1.shape
    tm = TM_POST
    tok = lambda w: pl.BlockSpec((tm, w), lambda i: (i, 0))
    y_k = lambda k: pl.BlockSpec((tm,) + yt.shape[1:], lambda i: (k * (n // tm) + i, 0, 0))
    return pl.pallas_call(
        _combine_kernel,
        out_shape=jax.ShapeDtypeStruct((n, d), F32),
        grid=(n // tm,),
        in_specs=[y_k(k) for k in range(TOP_K)]
        + [tok(TOP_K), tok(d), pl.BlockSpec((1, 1, mod3.shape[2]), lambda i: (i // tiles_per_seq, 0, 0)),
           pl.BlockSpec(g_post.shape, lambda i: (0, 0))],
        out_specs=tok(d),
        compiler_params=_cparams(("arbitrary",)),
        name="combine",
    )(*([yt] * TOP_K), gates, x1, mod3, g_post)


def _layer(x, mod, positions, g_pre_mix, g_post_mix, g_pre_ffn, g_post_ffn, w_in, g_q_a, g_kv_a,
           w_q_b, w_kv_b, w_o, w_router, b_router, w_mlp1, b_mlp1, w_mlp2, b_mlp2):
    b, s, d = x.shape
    n = b * s
    mod3 = mod.reshape(b, 1, mod.shape[1])
    row = lambda g: g.reshape(1, -1)

    win, wq, wkv = _relayout_in_weights(w_in, w_q_b, w_kv_b)
    qm, km, vm, qd, kd, vd, qi, kiw = _inproj(
        x, mod3, positions.reshape(b, s, 1), row(g_pre_mix), win, row(g_q_a), row(g_kv_a), wq, wkv,
        _rope_rows())
    y_mla = _mla(qm, km, vm)
    y_dsa = _dsa(qi, kiw, qd, kd, vd, positions.reshape(b, s, 1),
                 positions.reshape(b, s // QB_DSA, 1, QB_DSA))

    wo = w_o.astype(BF16)
    split = MLA_HEADS * MLA_V
    x1, h2, meta, gates = _post(
        y_mla.reshape(n, -1), y_dsa.reshape(n, -1), x.reshape(n, d), mod3, wo[:split], wo[split:],
        row(g_post_mix), row(g_pre_ffn), w_router, row(b_router), s // TM_POST)

    n_slots = n * TOP_K + N_EXPERTS * BM_FFN
    dest, blk = _route(meta, n_slots // BM_FFN)
    dest_flat = dest.reshape(-1)
    xs, code, w1g, w1l = _dispatch(dest_flat, blk[1], blk[2, 0:1], h2, w_mlp1, n_slots)
    f = w_mlp2.shape[1]
    yt = _ffn(blk[0], blk[1], blk[2, 0:1], code, xs, w1g, w1l, w_mlp2,
              b_mlp1[:, 0::2].reshape(-1, 1, f), b_mlp1[:, 1::2].reshape(-1, 1, f),
              b_mlp2.reshape(-1, 1, d), n * TOP_K)
    out = _combine(yt, gates, x1, mod3, row(g_post_ffn), s // TM_POST)
    return out.reshape(b, s, d)


def kernel(x, c, positions, w_ada, b_ada, g_pre_mix, g_post_mix, g_pre_ffn, g_post_ffn, w_in, g_q_a, g_kv_a, w_q_b, w_kv_b, w_o, w_router, b_router, w_mlp1, b_mlp1, w_mlp2, b_mlp2):
    for l in range(w_ada.shape[0]):
        mod = _ada(c, w_ada[l], b_ada[l])
        x = _layer(x, mod, positions, g_pre_mix[l], g_post_mix[l], g_pre_ffn[l], g_post_ffn[l],
                   w_in[l], g_q_a[l], g_kv_a[l], w_q_b[l], w_kv_b[l], w_o[l], w_router[l], b_router[l],
                   w_mlp1[l], b_mlp1[l], w_mlp2[l], b_mlp2[l])
    return x
```

```python
import functools

import jax
import jax.numpy as jnp
from jax import lax
from jax.experimental import pallas as pl
from jax.experimental.pallas import tpu as pltpu

F32 = jnp.float32
BF16 = jnp.bfloat16
I32 = jnp.int32

MLA_HEADS = 8
MLA_Q_RANK = 256
MLA_KV_RANK = 128
MLA_NOPE = 64
MLA_ROPE = 32
MLA_V = 64
ROPE_THETA = 10000.0
DSA_HEADS = 8
DSA_KV_HEADS = 2
DSA_HEAD_DIM = 64
IDX_HEADS = 8
IDX_DIM = 32
IDX_TOPK_MAX = 256
N_EXPERTS = 32
TOP_K = 4
SWIGLU_ALPHA = 1.702
SWIGLU_LIMIT = 7.0
NORM_EPS = 1e-6
NEG_INF = -1e30
LOG2E = 1.4426950408889634

LANES = 128
HEAD_PAD = 128
VMEM_LIMIT = 52 * 1024 * 1024

TM_PROJ = 512
KV_CHUNK = 256
TQ_MLA = 512
QB_DSA = 128
TM_POST = 512
BM_FFN = 256
TM_ROWS = 256

EMPTY_SLOT = -1

_NT = (((1,), (1,)), ((), ()))


def _rms(x, g):
    ms = jnp.mean(x * x, axis=-1, keepdims=True)
    return x * lax.rsqrt(ms + NORM_EPS) * g


def _cparams(sem, vmem=VMEM_LIMIT):
    return pltpu.CompilerParams(dimension_semantics=sem, vmem_limit_bytes=vmem)


def _ada_kernel(c_ref, w_ref, b_ref, o_ref):
    c = c_ref[...]
    cond = c * jax.nn.sigmoid(c)
    o_ref[...] = jnp.dot(cond.astype(BF16), w_ref[...].astype(BF16),
                         preferred_element_type=F32) + b_ref[...]


def _ada(c, w_ada, b_ada):
    b, d = c.shape
    n = w_ada.shape[1]
    tn = 1024
    return pl.pallas_call(
        _ada_kernel,
        out_shape=jax.ShapeDtypeStruct((b, n), F32),
        grid=(n // tn,),
        in_specs=[pl.BlockSpec((b, d), lambda i: (0, 0)),
                  pl.BlockSpec((d, tn), lambda i: (0, i)),
                  pl.BlockSpec((1, tn), lambda i: (0, i))],
        out_specs=pl.BlockSpec((b, tn), lambda i: (0, i)),
        compiler_params=_cparams(("arbitrary",)),
        name="ada",
    )(c, w_ada, b_ada.reshape(1, n))


_O_CQ = 0
_O_CKV = _O_CQ + MLA_Q_RANK
_O_KRA = _O_CKV + MLA_KV_RANK
_O_KRB = _O_KRA + HEAD_PAD
_O_QD = _O_KRB + HEAD_PAD
_O_KD = _O_QD + DSA_HEADS * DSA_HEAD_DIM
_O_VD = _O_KD + DSA_KV_HEADS * DSA_HEAD_DIM
_O_QI = _O_VD + DSA_KV_HEADS * DSA_HEAD_DIM
_O_KIW = _O_QI + IDX_HEADS * IDX_DIM
_W_IN = _O_KIW + LANES


def _inproj_kernel(x_ref, mod_ref, pos_ref, gpre_ref, win_ref, gq_ref, gkv_ref, wq_ref, wkv_ref,
                   rope_ref, qm_ref, km_ref, vm_ref, qd_ref, kd_ref, vd_ref, qi_ref, kiw_ref):
    d = x_ref.shape[2]
    x = x_ref[0]
    sh1 = mod_ref[0, :, 0:d]
    sc1 = mod_ref[0, :, d:2 * d]
    h = _rms(x, gpre_ref[...]) * (1.0 + sc1) + sh1
    proj = jnp.dot(h.astype(BF16), win_ref[...], preferred_element_type=F32)

    ang = pos_ref[0].astype(F32) * rope_ref[0:1, :]
    cos_t = jnp.cos(ang) * rope_ref[1:2, :] + rope_ref[3:4, :]
    sin_t = jnp.sin(ang) * rope_ref[2:3, :]

    hw = MLA_HEADS * HEAD_PAD
    nq = _rms(proj[:, _O_CQ:_O_CQ + MLA_Q_RANK], gq_ref[...])
    qab = jnp.dot(nq.astype(BF16), wq_ref[...], preferred_element_type=F32)
    scale = (MLA_NOPE + MLA_ROPE) ** -0.5 * LOG2E
    nkv = _rms(proj[:, _O_CKV:_O_CKV + MLA_KV_RANK], gkv_ref[...])
    kv = jnp.dot(nkv.astype(BF16), wkv_ref[...], preferred_element_type=F32)
    kr = proj[:, _O_KRA:_O_KRA + HEAD_PAD] * cos_t + proj[:, _O_KRB:_O_KRB + HEAD_PAD] * sin_t
    for hd in range(MLA_HEADS):
        sl = slice(hd * HEAD_PAD, (hd + 1) * HEAD_PAD)
        slb = slice(hw + hd * HEAD_PAD, hw + (hd + 1) * HEAD_PAD)
        qm_ref[0, :, sl] = ((qab[:, sl] * cos_t + qab[:, slb] * sin_t) * scale).astype(BF16)
        km_ref[0, :, sl] = (kv[:, sl] + kr).astype(BF16)

    qd_ref[0] = (proj[:, _O_QD:_O_KD] * (DSA_HEAD_DIM ** -0.5 * LOG2E)).astype(BF16)
    kd_ref[0] = proj[:, _O_KD:_O_VD].astype(BF16)
    qi_ref[0] = (proj[:, _O_QI:_O_KIW] * (IDX_DIM ** -0.5)).astype(BF16)
    kiw_ref[0] = proj[:, _O_KIW:_W_IN]
    ck = vm_ref.shape[3]
    for t in range(x_ref.shape[1] // ck):
        rows = slice(t * ck, (t + 1) * ck)
        vm_ref[0, t] = kv[rows, hw:hw + MLA_HEADS * MLA_V].T.astype(BF16)
        vd_ref[0, t] = proj[rows, _O_VD:_O_QI].T.astype(BF16)


def _relayout_in_weights(w_in, w_q_b, w_kv_b):
    d = w_in.shape[0]
    half = MLA_ROPE // 2
    o = 0
    segs = {}
    for name, width in (("cq", MLA_Q_RANK), ("ckv", MLA_KV_RANK), ("kr", MLA_ROPE),
                        ("qd", DSA_HEADS * DSA_HEAD_DIM), ("kd", DSA_KV_HEADS * DSA_HEAD_DIM),
                        ("vd", DSA_KV_HEADS * DSA_HEAD_DIM), ("qi", IDX_HEADS * IDX_DIM),
                        ("ki", IDX_DIM), ("wi", IDX_HEADS)):
        segs[name] = w_in[:, o:o + width]
        o += width
    z = lambda n: jnp.zeros((d, n), w_in.dtype)
    x1, x2 = segs["kr"][:, :half], segs["kr"][:, half:]
    tail = HEAD_PAD - MLA_NOPE - MLA_ROPE
    kra = jnp.concatenate([z(MLA_NOPE), x1, x2, z(tail)], axis=1)
    krb = jnp.concatenate([z(MLA_NOPE), x2, x1, z(tail)], axis=1)
    win = jnp.concatenate([segs["cq"], segs["ckv"], kra, krb, segs["qd"], segs["kd"], segs["vd"],
                           segs["qi"], segs["ki"], segs["wi"], z(LANES - IDX_DIM - IDX_HEADS)], axis=1)
    r = w_q_b.shape[0]
    wq = w_q_b.reshape(r, MLA_HEADS, MLA_NOPE + MLA_ROPE)
    zq = lambda n: jnp.zeros((r, MLA_HEADS, n), w_q_b.dtype)
    wqa = jnp.concatenate([wq, zq(tail)], axis=2).reshape(r, MLA_HEADS * HEAD_PAD)
    wqb = jnp.concatenate([zq(MLA_NOPE), wq[:, :, MLA_NOPE + half:], wq[:, :, MLA_NOPE:MLA_NOPE + half],
                           zq(tail)], axis=2).reshape(r, MLA_HEADS * HEAD_PAD)
    rk = w_kv_b.shape[0]
    wkv = w_kv_b.reshape(rk, MLA_HEADS, MLA_NOPE + MLA_V)
    wkn = jnp.concatenate([wkv[:, :, :MLA_NOPE], jnp.zeros((rk, MLA_HEADS, HEAD_PAD - MLA_NOPE), w_kv_b.dtype)],
                          axis=2).reshape(rk, MLA_HEADS * HEAD_PAD)
    wv = wkv[:, :, MLA_NOPE:].reshape(rk, MLA_HEADS * MLA_V)
    return (win.astype(BF16), jnp.concatenate([wqa, wqb], axis=1).astype(BF16),
            jnp.concatenate([wkn, wv], axis=1).astype(BF16))


def _rope_rows():
    half = MLA_ROPE // 2
    lane = jnp.arange(LANES)
    freqs = ROPE_THETA ** (-jnp.arange(half, dtype=F32) / half)
    in_x1 = (lane >= MLA_NOPE) & (lane < MLA_NOPE + half)
    in_x2 = (lane >= MLA_NOPE + half) & (lane < MLA_NOPE + MLA_ROPE)
    fr = jnp.where(in_x1 | in_x2, freqs[(lane - MLA_NOPE) % half], 0.0)
    cosm = (in_x1 | in_x2).astype(F32)
    sinm = jnp.where(in_x1, -1.0, jnp.where(in_x2, 1.0, 0.0))
    nopem = (lane < MLA_NOPE).astype(F32)
    rows = jnp.stack([fr, cosm, sinm, nopem], axis=0).astype(F32)
    return jnp.concatenate([rows, jnp.zeros((4, LANES), F32)], axis=0)


def _inproj(x, mod3, pos3, g_pre, win, g_q, g_kv, wq, wkv, rope_rows):
    b, s, d = x.shape
    tm = TM_PROJ
    ck = KV_CHUNK
    tok = lambda w: pl.BlockSpec((1, tm, w), lambda bi, i: (bi, i, 0))
    full = lambda a: pl.BlockSpec(a.shape, lambda bi, i: (0,) * a.ndim)
    tr = lambda w: pl.BlockSpec((1, tm // ck, w, ck), lambda bi, i: (bi, i, 0, 0))
    tok_out = lambda w, dt: (jax.ShapeDtypeStruct((b, s, w), dt), tok(w))
    tr_out = lambda w: (jax.ShapeDtypeStruct((b, s // ck, w, ck), BF16), tr(w))
    outs = [tok_out(MLA_HEADS * HEAD_PAD, BF16), tok_out(MLA_HEADS * HEAD_PAD, BF16),
            tr_out(MLA_HEADS * MLA_V), tok_out(DSA_HEADS * DSA_HEAD_DIM, BF16),
            tok_out(DSA_KV_HEADS * DSA_HEAD_DIM, BF16), tr_out(DSA_KV_HEADS * DSA_HEAD_DIM),
            tok_out(IDX_HEADS * IDX_DIM, BF16), tok_out(LANES, F32)]
    return pl.pallas_call(
        _inproj_kernel,
        out_shape=[o[0] for o in outs],
        grid=(b, s // tm),
        in_specs=[tok(d),
                  pl.BlockSpec((1, 1, mod3.shape[2]), lambda bi, i: (bi, 0, 0)),
                  tok(1), full(g_pre), full(win), full(g_q), full(g_kv), full(wq), full(wkv),
                  full(rope_rows)],
        out_specs=[o[1] for o in outs],
        compiler_params=_cparams(("parallel", "arbitrary")),
        name="inproj",
    )(x, mod3, pos3, g_pre, win, g_q, g_kv, wq, wkv, rope_rows)


def _mla_kernel(q_ref, k_ref, vt_ref, o_ref, m_ref, l_ref, acc_ref):
    j = pl.program_id(1)
    tq = q_ref.shape[1]
    tk = k_ref.shape[1] // vt_ref.shape[1]
    m_ref[...] = jnp.full_like(m_ref, NEG_INF)
    l_ref[...] = jnp.zeros_like(l_ref)
    acc_ref[...] = jnp.zeros_like(acc_ref)
    per_q = tq // tk
    krow = lax.broadcasted_iota(I32, (tk, 1), 0)
    qcol = j * tq + lax.broadcasted_iota(I32, (1, tq), 1)

    def chunk(c, diagonal):
        off = pl.multiple_of(c * tk, tk)
        m_all = m_ref[...]
        l_all = l_ref[...]
        accs = [acc_ref[hd] for hd in range(MLA_HEADS)]
        ms, ls = [], []

        def scores(hd):
            lanes = slice(hd * HEAD_PAD, (hd + 1) * HEAD_PAD)
            return lax.dot_general(k_ref[0, pl.ds(off, tk), lanes], q_ref[0, :, lanes], _NT,
                                   preferred_element_type=F32)

        s_next = scores(0)
        pend = None
        for hd in range(MLA_HEADS):
            s = s_next
            if hd + 1 < MLA_HEADS:
                s_next = scores(hd + 1)
            if diagonal:
                s = jnp.where(off + krow <= qcol, s, NEG_INF)
            m_old = m_all[hd:hd + 1, :]
            m_new = jnp.maximum(m_old, jnp.max(s, axis=0, keepdims=True))
            alpha = jnp.exp2(m_old - m_new)
            p = jnp.exp2(s - m_new)
            ls.append(alpha * l_all[hd:hd + 1, :] + jnp.sum(p, axis=0, keepdims=True))
            ms.append(m_new)
            pv = jnp.dot(vt_ref[0, c, hd * MLA_V:(hd + 1) * MLA_V, :], p.astype(BF16),
                         preferred_element_type=F32)
            if pend is not None:
                accs[pend[0]] = pend[1] * accs[pend[0]] + pend[2]
            pend = (hd, alpha, pv)
        accs[pend[0]] = pend[1] * accs[pend[0]] + pend[2]
        m_ref[...] = jnp.concatenate(ms, axis=0)
        l_ref[...] = jnp.concatenate(ls, axis=0)
        for hd in range(MLA_HEADS):
            acc_ref[hd] = accs[hd]

    def body(c, carry):
        chunk(c, False)
        return carry

    lax.fori_loop(0, j * per_q, body, 0)
    for t in range(per_q):
        chunk(j * per_q + t, True)
    o_t = jnp.concatenate([acc_ref[hd] / l_ref[hd:hd + 1, :] for hd in range(MLA_HEADS)], axis=0)
    o_ref[0] = o_t.T.astype(BF16)


def _mla(qm, km, vmt):
    b, s, hw = qm.shape
    tq = TQ_MLA
    _, nck, vw, ck = vmt.shape
    return pl.pallas_call(
        _mla_kernel,
        out_shape=jax.ShapeDtypeStruct((b, s, vw), BF16),
        grid=(b, s // tq),
        in_specs=[pl.BlockSpec((1, tq, hw), lambda bi, j: (bi, j, 0)),
                  pl.BlockSpec((1, s, hw), lambda bi, j: (bi, 0, 0)),
                  pl.BlockSpec((1, nck, vw, ck), lambda bi, j: (bi, 0, 0, 0))],
        out_specs=pl.BlockSpec((1, tq, vw), lambda bi, j: (bi, j, 0)),
        scratch_shapes=[pltpu.VMEM((MLA_HEADS, tq), F32), pltpu.VMEM((MLA_HEADS, tq), F32),
                        pltpu.VMEM((MLA_HEADS, MLA_V, tq), F32)],
        compiler_params=_cparams(("parallel", "arbitrary")),
        name="mla",
    )(qm, km, vmt)


_KEY_NEG_INF = -2139095041
_KEY_POS_INF = 2139095040
_I32_MAX = 2147483647


def _key_to_f32(k):
    bits = k ^ ((k >> 31) & _I32_MAX)
    return lax.bitcast_convert_type(bits, F32)


def _dsa_kernel(qi_ref, kiw_ref, qd_ref, kd_ref, vdt_ref, posk_ref, posq_ref, y_ref, sc_ref, acc_ref):
    j = pl.program_id(1)
    qb, ck = QB_DSA, KV_CHUNK
    nch = (j * qb + qb + ck - 1) // ck
    n_sel = IDX_TOPK_MAX

    kiw_q = kiw_ref[0, pl.ds(pl.multiple_of(j * qb, qb), qb), :]
    w_t = kiw_q.T[IDX_DIM:IDX_DIM + IDX_HEADS, :] * (IDX_HEADS ** -0.5)
    qi = qi_ref[0]
    qi_stack = jnp.concatenate([qi[:, hd * IDX_DIM:(hd + 1) * IDX_DIM] for hd in range(IDX_HEADS)], axis=0)
    q_idx = j * qb + lax.broadcasted_iota(I32, (1, qb), 1)
    k_iota = lax.broadcasted_iota(I32, (ck, 1), 0)

    def idx_chunk(c, carry):
        off = pl.multiple_of(c * ck, ck)
        ki = kiw_ref[0, pl.ds(off, ck), :][:, 0:IDX_DIM].astype(BF16)
        r = lax.dot_general(ki, qi_stack, _NT, preferred_element_type=F32)
        acc = jnp.zeros((ck, qb), F32)
        for hd in range(IDX_HEADS):
            acc = acc + w_t[hd:hd + 1, :] * jnp.maximum(r[:, hd * qb:(hd + 1) * qb], 0.0)
        sc_ref[pl.ds(off, ck), :] = jnp.where(off + k_iota <= q_idx, acc, NEG_INF)
        return carry

    lax.fori_loop(0, nch, idx_chunk, 0)

    n_part = 4

    def count(pred_fn):
        def body(c, parts):
            off = pl.multiple_of(c * ck, ck)
            hit = pred_fn(sc_ref[pl.ds(off, ck), :], off).astype(F32)
            rows = ck // n_part
            return tuple(p + jnp.sum(hit[i * rows:(i + 1) * rows].reshape(rows // 8, 8, qb), axis=0)
                         for i, p in enumerate(parts))
        parts = lax.fori_loop(0, nch, body, (jnp.zeros((8, qb), F32),) * n_part)
        return jnp.sum((parts[0] + parts[1]) + (parts[2] + parts[3]), axis=0, keepdims=True)

    def no_tie(_):
        return jnp.full((1, qb), jnp.inf, F32), jnp.full((1, qb), _I32_MAX, I32)

    def select(_):
        def bisect(_, st):
            lo, hi, c_lo = st
            mid = (lo & hi) + ((lo ^ hi) >> 1)
            t = _key_to_f32(mid)
            cnt = count(lambda x, off: x >= t)
            ok = cnt >= n_sel
            return jnp.where(ok, mid, lo), jnp.where(ok, hi, mid), jnp.where(ok, cnt, c_lo)

        per_round = 4

        def round_(st):
            it, lo, hi, c_lo = st
            lo, hi, c_lo = lax.fori_loop(0, per_round, bisect, (lo, hi, c_lo))
            return it + per_round, lo, hi, c_lo

        def unsettled(st):
            it, _, _, c_lo = st
            return (it < 32) & (jnp.max(jnp.abs(c_lo - n_sel)) > 0.5)

        st0 = (jnp.int32(0), jnp.full((1, qb), _KEY_NEG_INF, I32), jnp.full((1, qb), _KEY_POS_INF, I32),
               jnp.broadcast_to((nch * ck).astype(F32), (1, qb)))
        _, lo, _, c_lo = lax.while_loop(unsettled, round_, st0)
        t_lo = _key_to_f32(lo)

        def tie_search(_):
            t_hi = _key_to_f32(lo + 1)
            need = n_sel - count(lambda x, off: x >= t_hi)

            def step(_, lohi):
                mlo, mhi = lohi
                mid = (mlo + mhi) >> 1
                cnt = count(lambda x, off: (x >= t_lo) & jnp.logical_not(x >= t_hi)
                            & (off + k_iota <= mid))
                ok = cnt >= need
                return jnp.where(ok, mlo, mid), jnp.where(ok, mid, mhi)
            mlo0 = jnp.full((1, qb), -1, I32)
            mhi0 = jnp.full((1, qb), sc_ref.shape[0] - 1, I32)
            return t_hi, lax.fori_loop(0, 12, step, (mlo0, mhi0))[1]

        tied = jnp.max(jnp.abs(c_lo - n_sel)) > 0.5
        t_hi, m_sel = lax.cond(tied, tie_search, no_tie, 0)
        return t_lo, t_hi, m_sel

    def all_causal(_):
        return (jnp.full((1, qb), 0.1 * NEG_INF, F32), jnp.full((1, qb), jnp.inf, F32),
                jnp.full((1, qb), _I32_MAX, I32))

    t_lo, t_hi, m_sel = lax.cond((j + 1) * qb > n_sel, select, all_causal, 0)

    grp = DSA_HEADS // DSA_KV_HEADS
    dh = DSA_HEAD_DIM
    qd = qd_ref[0]
    q_groups = [jnp.concatenate([qd[:, (g * grp + i) * dh:(g * grp + i + 1) * dh] for i in range(grp)], axis=0)
                for g in range(DSA_KV_HEADS)]
    posq = posq_ref[0, j]
    acc_ref[...] = jnp.zeros_like(acc_ref)

    def att_chunk(c, carry):
        ms, ls = carry
        off = pl.multiple_of(c * ck, ck)
        x = sc_ref[pl.ds(off, ck), :]
        sel = (x >= t_hi) | ((x >= t_lo) & (off + k_iota <= m_sel))
        dist = jnp.abs(posk_ref[0, pl.ds(off, ck), :] - posq).astype(F32)
        kd = kd_ref[0, pl.ds(off, ck), :]
        new_ms, new_ls = [], []
        s_all = [lax.dot_general(kd[:, g * dh:(g + 1) * dh], q_groups[g], _NT,
                                 preferred_element_type=F32) for g in range(DSA_KV_HEADS)]
        for g in range(DSA_KV_HEADS):
            s = s_all[g]
            ps, alphas = [], []
            for i in range(grp):
                hd = g * grp + i
                slope = 2.0 ** (-8.0 * (hd + 1) / DSA_HEADS) * LOG2E
                si = jnp.where(sel, s[:, i * qb:(i + 1) * qb] - slope * dist, NEG_INF)
                m_new = jnp.maximum(ms[hd], jnp.max(si, axis=0, keepdims=True))
                alpha = jnp.exp2(ms[hd] - m_new)
                p = jnp.exp2(si - m_new)
                new_ls.append(alpha * ls[hd] + jnp.sum(p, axis=0, keepdims=True))
                new_ms.append(m_new)
                ps.append(p.astype(BF16))
                alphas.append(alpha)
            pv = jnp.dot(vdt_ref[0, c, g * dh:(g + 1) * dh, :], jnp.concatenate(ps, axis=1),
                         preferred_element_type=F32)
            acc_ref[g] = jnp.concatenate(alphas, axis=1) * acc_ref[g] + pv
        return tuple(new_ms), tuple(new_ls)

    init = ((jnp.full((1, qb), NEG_INF, F32),) * DSA_HEADS, (jnp.zeros((1, qb), F32),) * DSA_HEADS)
    _, ls = lax.fori_loop(0, nch, att_chunk, init)
    o_t = jnp.concatenate([acc_ref[hd // grp][:, (hd % grp) * qb:(hd % grp + 1) * qb] / ls[hd]
                           for hd in range(DSA_HEADS)], axis=0)
    y_ref[0] = o_t.T.astype(BF16)


def _dsa(qi, kiw, qd, kd, vdt, posk, posq):
    b, s, _ = qd.shape
    qb = QB_DSA
    _, nck, vw, ck = vdt.shape
    grp = DSA_HEADS // DSA_KV_HEADS
    blk = lambda w: pl.BlockSpec((1, qb, w), lambda bi, j: (bi, j, 0))
    seq = lambda w: pl.BlockSpec((1, s, w), lambda bi, j: (bi, 0, 0))
    return pl.pallas_call(
        _dsa_kernel,
        out_shape=jax.ShapeDtypeStruct((b, s, qd.shape[2]), BF16),
        grid=(b, s // qb),
        in_specs=[blk(qi.shape[2]), seq(kiw.shape[2]), blk(qd.shape[2]), seq(kd.shape[2]),
                  pl.BlockSpec((1, nck, vw, ck), lambda bi, j: (bi, 0, 0, 0)),
                  seq(1),
                  pl.BlockSpec((1, s // qb, 1, qb), lambda bi, j: (bi, 0, 0, 0))],
        out_specs=blk(qd.shape[2]),
        scratch_shapes=[pltpu.VMEM((s, qb), F32),
                        pltpu.VMEM((DSA_KV_HEADS, DSA_HEAD_DIM, grp * qb), F32)],
        compiler_params=_cparams(("parallel", "arbitrary")),
        name="dsa",
    )(qi, kiw, qd, kd, vdt, posk, posq)


_META_ROWS = 16


def _post_kernel(ym_ref, yd_ref, x_ref, mod_ref, woa_ref, wob_ref, gpost_ref, gpre_ref, wr_ref, br_ref,
                 x1_ref, h2_ref, meta_ref, gates_ref, cnt_ref):
    i = pl.program_id(0)
    d = x_ref.shape[1]
    tm = x_ref.shape[0]

    @pl.when(i == 0)
    def _():
        cnt_ref[...] = jnp.zeros_like(cnt_ref)

    mix = (jnp.dot(ym_ref[...], woa_ref[...], preferred_element_type=F32)
           + jnp.dot(yd_ref[...], wob_ref[...], preferred_element_type=F32))
    gt1 = mod_ref[0, :, 2 * d:3 * d]
    sh2 = mod_ref[0, :, 3 * d:4 * d]
    sc2 = mod_ref[0, :, 4 * d:5 * d]
    x1 = x_ref[...] + gt1 * _rms(mix, gpost_ref[...])
    x1_ref[...] = x1
    h2 = _rms(x1, gpre_ref[...]) * (1.0 + sc2) + sh2
    h2_ref[...] = h2.reshape(h2_ref.shape)

    ne = br_ref.shape[1]
    h_hi = h2.astype(BF16)
    h_lo = (h2 - h_hi.astype(F32)).astype(BF16)
    hw = jnp.dot(h_hi, wr_ref[...], preferred_element_type=F32)
    logits = (hw[:, :ne] + hw[:, ne:] + jnp.dot(h_lo, wr_ref[:, :ne], preferred_element_type=F32)
              + br_ref[...])
    lane = lax.broadcasted_iota(I32, (tm, ne), 1).astype(F32)
    work = logits
    ids, vals = [], []
    for _ in range(TOP_K):
        mx = jnp.max(work, axis=-1, keepdims=True)
        idx = jnp.min(jnp.where(work == mx, lane, float(ne)), axis=-1, keepdims=True)
        ids.append(idx)
        vals.append(mx)
        work = jnp.where(lane == idx, -jnp.inf, work)
    es = [jnp.exp(v - vals[0]) for v in vals]
    den = es[0] + es[1] + es[2] + es[3]
    gates = [e / den for e in es]

    member = jnp.zeros((tm, ne), F32)
    for idx in ids:
        member = member + (lane == idx).astype(F32)
    r_i = lax.broadcasted_iota(I32, (tm, tm), 0)
    c_i = lax.broadcasted_iota(I32, (tm, tm), 1)
    tri = (c_i < r_i).astype(BF16)
    before = jnp.dot(tri, member.astype(BF16), preferred_element_type=F32) + cnt_ref[...]
    cnt_ref[...] = cnt_ref[...] + jnp.sum(member, axis=0, keepdims=True)
    ranks = [jnp.sum(jnp.where(lane == idx, before, 0.0), axis=-1, keepdims=True) for idx in ids]

    lane_m = lax.broadcasted_iota(I32, (tm, LANES), 1)
    rec = jnp.zeros((tm, LANES), F32)
    for k, col in enumerate(ids + gates + ranks):
        rec = jnp.where(lane_m == k, col, rec)
    meta_ref[...] = rec.T[0:_META_ROWS, :]
    lane_g = lax.broadcasted_iota(I32, (tm, TOP_K), 1)
    gm = jnp.zeros((tm, TOP_K), F32)
    for k, col in enumerate(gates):
        gm = jnp.where(lane_g == k, col, gm)
    gates_ref[...] = gm


def _post(ym, yd, x2, mod3, woa, wob, g_post, g_pre, w_router, b_router, tiles_per_seq):
    n, d = x2.shape
    tm = TM_POST
    ne = b_router.shape[1]
    w_hi = w_router.astype(BF16)
    w_router = jnp.concatenate([w_hi, (w_router - w_hi.astype(F32)).astype(BF16)], axis=1)
    tok = lambda w: pl.BlockSpec((tm, w), lambda i: (i, 0))
    full = lambda a: pl.BlockSpec(a.shape, lambda i: (0,) * a.ndim)
    return pl.pallas_call(
        _post_kernel,
        out_shape=[jax.ShapeDtypeStruct((n, d), F32), jax.ShapeDtypeStruct((n, d // LANES, LANES), F32),
                   jax.ShapeDtypeStruct((_META_ROWS, n), F32), jax.ShapeDtypeStruct((n, TOP_K), F32)],
        grid=(n // tm,),
        in_specs=[tok(ym.shape[1]), tok(yd.shape[1]), tok(d),
                  pl.BlockSpec((1, 1, mod3.shape[2]), lambda i: (i // tiles_per_seq, 0, 0)),
                  full(woa), full(wob), full(g_post), full(g_pre), full(w_router), full(b_router)],
        out_specs=[tok(d), pl.BlockSpec((tm, d // LANES, LANES), lambda i: (i, 0, 0)),
                   pl.BlockSpec((_META_ROWS, tm), lambda i: (0, i)), tok(TOP_K)],
        scratch_shapes=[pltpu.VMEM((1, ne), F32)],
        compiler_params=_cparams(("arbitrary",)),
        name="post",
    )(ym, yd, x2, mod3, woa, wob, g_post, g_pre, w_router, b_router)


def _route_kernel(meta_ref, dest_ref, blk_ref):
    ne = N_EXPERTS
    nbl = blk_ref.shape[1]
    e_col = lax.broadcasted_iota(I32, (ne, 1), 0).astype(F32)
    onehots = [(meta_ref[k:k + 1, :] == e_col) for k in range(TOP_K)]
    counts = jnp.zeros((ne, 1), F32)
    for oh in onehots:
        counts = counts + jnp.sum(oh.astype(F32), axis=1, keepdims=True)
    padded = jnp.floor((counts + float(BM_FFN - 1)) / BM_FFN) * BM_FFN
    r_i = lax.broadcasted_iota(I32, (ne, ne), 0)
    c_i = lax.broadcasted_iota(I32, (ne, ne), 1)
    tri = (c_i < r_i).astype(F32)
    padded_f = jnp.broadcast_to(padded, (ne, LANES))
    start = jnp.dot(tri, padded_f, preferred_element_type=F32, precision=lax.Precision.HIGHEST)[:, 0:1]
    for k in range(TOP_K):
        add = jnp.sum(jnp.where(onehots[k], start, 0.0), axis=0, keepdims=True)
        dest_ref[k:k + 1, :] = (meta_ref[2 * TOP_K + k:2 * TOP_K + k + 1, :] + add).astype(I32)
    end = start + padded
    row0 = (lax.broadcasted_iota(I32, (1, nbl), 1) * BM_FFN).astype(F32)
    blk_e = jnp.minimum(jnp.sum((end <= row0).astype(F32), axis=0, keepdims=True), float(ne - 1))
    own = blk_e == e_col
    real_end = jnp.sum(jnp.where(own, start + counts, 0.0), axis=0, keepdims=True)
    nvalid = jnp.clip(real_end - row0, 0.0, float(BM_FFN))
    nact = jnp.sum(padded, axis=0, keepdims=True) / BM_FFN
    rows = lax.broadcasted_iota(I32, blk_ref.shape, 0)
    out = jnp.where(rows == 0, blk_e, jnp.where(rows == 1, nvalid, jnp.broadcast_to(nact, blk_ref.shape)))
    blk_ref[...] = out.astype(I32)


def _route(meta, n_blocks):
    n = meta.shape[1]
    nbl = (n_blocks + LANES - 1) // LANES * LANES
    return pl.pallas_call(
        _route_kernel,
        out_shape=[jax.ShapeDtypeStruct((TOP_K, n), I32), jax.ShapeDtypeStruct((8, nbl), I32)],
        compiler_params=pltpu.CompilerParams(vmem_limit_bytes=VMEM_LIMIT),
        name="route",
    )(meta)


def _dispatch_kernel(dest_ref, nv_ref, na_ref, h_ref, w_ref, xs_ref, code_ref, g_ref, l_ref, zero_ref, t_ref,
                     sem, zsem):
    i = pl.program_id(0)
    tm = h_ref.shape[0]
    n = pl.num_programs(0) * tm
    bm = zero_ref.shape[0]

    @pl.when(i == 0)
    def _():
        zero_ref[...] = jnp.zeros_like(zero_ref)

        def fill(b, go):
            copy = pltpu.make_async_copy(zero_ref, xs_ref.at[pl.ds(b * bm, bm)], zsem)

            @pl.when((b >= na_ref[0]) | (nv_ref[b] < bm))
            def _():
                copy.start() if go else copy.wait()

        def mark_empty(b, c):
            first = jnp.where(b >= na_ref[0], 0, nv_ref[b])

            def mark(r, c2):
                code_ref[b * bm + r] = EMPTY_SLOT
                return c2
            return lax.fori_loop(first, bm, mark, c)

        n_blocks = xs_ref.shape[0] // bm
        lax.fori_loop(0, n_blocks, lambda b, c: (fill(b, True), c)[1], 0)
        lax.fori_loop(0, n_blocks, mark_empty, 0)
        lax.fori_loop(0, n_blocks, lambda b, c: (fill(b, False), c)[1], 0)

    def row_copy(t, k):
        dst = dest_ref[k * n + i * tm + t]
        return pltpu.make_async_copy(h_ref.at[t], xs_ref.at[dst], sem)

    for t in range(tm):
        for k in range(TOP_K):
            code_ref[dest_ref[k * n + i * tm + t]] = k * n + i * tm + t
            row_copy(t, k).start(priority=(t * TOP_K + k) % 2)
    _split_w1_block(w_ref, g_ref, l_ref, t_ref)
    for t in range(tm):
        for k in range(TOP_K):
            row_copy(t, k).wait()


def _split_w1_block(w_ref, g_ref, l_ref, t_ref):
    _, d, cw = w_ref.shape
    for r in range(d // LANES):
        rows = slice(r * LANES, (r + 1) * LANES)
        t_ref[r] = w_ref[0, rows, :].T
        g_ref[0, rows, :] = t_ref[r, pl.ds(0, cw // 2, stride=2), :].T.astype(BF16)
        l_ref[0, rows, :] = t_ref[r, pl.ds(1, cw // 2, stride=2), :].T.astype(BF16)


def _dispatch(dest_flat, blk_nv, n_act, h2, w1, n_slots):
    n, sub, ln = h2.shape
    e, d, f2 = w1.shape
    tm = TM_ROWS
    steps = n // tm
    per_e = steps // e
    cw = f2 // per_e
    assert per_e * e == steps and cw * per_e == f2 and cw % (2 * LANES) == 0
    w_out = jax.ShapeDtypeStruct((e, d, f2 // 2), BF16)
    slab = lambda width: pl.BlockSpec((1, d, width), lambda i, dest, nv, na: (i // per_e, 0, i % per_e))
    return pl.pallas_call(
        _dispatch_kernel,
        out_shape=[jax.ShapeDtypeStruct((n_slots, sub, ln), h2.dtype),
                   jax.ShapeDtypeStruct((n_slots,), I32), w_out, w_out],
        grid_spec=pltpu.PrefetchScalarGridSpec(
            num_scalar_prefetch=3,
            grid=(steps,),
            in_specs=[pl.BlockSpec((tm, sub, ln), lambda i, dest, nv, na: (i, 0, 0)), slab(cw)],
            out_specs=[pl.BlockSpec(memory_space=pl.ANY), pl.BlockSpec(memory_space=pltpu.SMEM),
                       slab(cw // 2), slab(cw // 2)],
            scratch_shapes=[pltpu.VMEM((BM_FFN, sub, ln), h2.dtype),
                            pltpu.VMEM((d // LANES, cw, LANES), F32),
                            pltpu.SemaphoreType.DMA, pltpu.SemaphoreType.DMA],
        ),
        compiler_params=_cparams(("arbitrary",)),
        name="dispatch",
    )(dest_flat, blk_nv, n_act, h2, w1)


def _ffn_kernel(be_ref, nv_ref, na_ref, code_ref, xs_ref, w1g_ref, w1l_ref, w2_ref, b1g_ref, b1l_ref, b2_ref,
                yt_ref, w2b_ref, stage_ref, sem):
    b = pl.program_id(0)
    steps = pl.num_programs(0)
    bm, sub, ln = xs_ref.shape
    n_rows = yt_ref.shape[0] - 2 * bm
    active = b < na_ref[0]

    def row_copy(step, slot, r, valid):
        code = jnp.where(valid, code_ref[step * bm + r], EMPTY_SLOT)
        dst = jnp.where(code == EMPTY_SLOT, n_rows + slot * bm + r, code)
        return pltpu.make_async_copy(stage_ref.at[slot, r], yt_ref.at[dst], sem.at[slot])

    def scatter(step, slot, go, valid=True):
        for r in range(bm):
            copy = row_copy(step, slot, r, valid)
            copy.start(priority=r % 2) if go else copy.wait()

    @pl.when(b == 0)
    def _():
        stage_ref[...] = jnp.zeros_like(stage_ref)
        spare = [pltpu.make_async_copy(stage_ref.at[p], yt_ref.at[pl.ds(n_rows + p * bm, bm)], sem.at[p])
                 for p in range(2)]
        for copy in spare:
            copy.start()
        for copy in spare:
            copy.wait()

    @pl.when(active & ((b == 0) | (be_ref[b] != be_ref[jnp.maximum(b - 1, 0)])))
    def _():
        w2b_ref[...] = w2_ref[0].astype(BF16)

    @pl.when(active)
    def _():
        prev = jnp.maximum(b - 1, 0)
        other = (b + 1) % 2
        scatter(prev, other, True, valid=b >= 1)
        rows = lax.broadcasted_iota(I32, (bm, 1), 0)
        x = jnp.where(rows < nv_ref[b], xs_ref[...].reshape(bm, sub * ln), 0.0).astype(BF16)
        glu = jnp.dot(x, w1g_ref[0], preferred_element_type=F32) + b1g_ref[0]
        lin = jnp.dot(x, w1l_ref[0], preferred_element_type=F32) + b1l_ref[0]
        glu = jnp.minimum(glu, SWIGLU_LIMIT)
        lin = jnp.clip(lin, -SWIGLU_LIMIT, SWIGLU_LIMIT)
        act = glu * jax.nn.sigmoid(SWIGLU_ALPHA * glu) * (lin + 1.0)
        y = jnp.dot(act.astype(BF16), w2b_ref[...], preferred_element_type=F32) + b2_ref[0]
        stage_ref[b % 2] = y.reshape(bm, sub, ln)
        scatter(prev, other, False, valid=b >= 1)

    @pl.when((b >= 1) & (b == na_ref[0]))
    def _():
        scatter(b - 1, (b - 1) % 2, True)
        scatter(b - 1, (b - 1) % 2, False)

    @pl.when(active & (b == steps - 1))
    def _():
        scatter(b, b % 2, True)
        scatter(b, b % 2, False)


def _ffn(blk_e, blk_nv, n_act, code, xs, w1g, w1l, w2, b1g, b1l, b2, n_rows):
    n_slots, sub, ln = xs.shape
    d = sub * ln
    bm = BM_FFN
    f = w1g.shape[2]
    last = lambda b, na: jnp.maximum(jnp.minimum(b, na[0] - 1), 0)
    blk = lambda b, be, nv, na, code: (last(b, na), 0, 0)
    exp3 = lambda b, be, nv, na, code: (be[last(b, na)], 0, 0)
    return pl.pallas_call(
        _ffn_kernel,
        out_shape=jax.ShapeDtypeStruct((n_rows + 2 * bm, sub, ln), F32),
        grid_spec=pltpu.PrefetchScalarGridSpec(
            num_scalar_prefetch=4,
            grid=(n_slots // bm,),
            in_specs=[pl.BlockSpec((bm, sub, ln), blk),
                      pl.BlockSpec((1, d, f), exp3), pl.BlockSpec((1, d, f), exp3),
                      pl.BlockSpec((1, f, d), exp3),
                      pl.BlockSpec((1, 1, f), exp3), pl.BlockSpec((1, 1, f), exp3),
                      pl.BlockSpec((1, 1, d), exp3)],
            out_specs=pl.BlockSpec(memory_space=pl.ANY),
            scratch_shapes=[pltpu.VMEM((f, d), BF16), pltpu.VMEM((2, bm, sub, ln), F32),
                            pltpu.SemaphoreType.DMA((2,))],
        ),
        compiler_params=_cparams(("arbitrary",)),
        name="ffn",
    )(blk_e, blk_nv, n_act, code, xs, w1g, w1l, w2, b1g, b1l, b2)


def _combine_kernel(*refs):
    y_refs, (gates_ref, x1_ref, mod_ref, gpost_ref, o_ref) = refs[:TOP_K], refs[TOP_K:]
    tm, d = x1_ref.shape
    g = gates_ref[...]
    ffn = g[:, 0:1] * y_refs[0][...].reshape(tm, d)
    for k in range(1, TOP_K):
        ffn = ffn + g[:, k:k + 1] * y_refs[k][...].reshape(tm, d)
    gt2 = mod_ref[0, :, 5 * d:6 * d]
    o_ref[...] = x1_ref[...] + gt2 * _rms(ffn, gpost_ref[...])


def _combine(yt, gates, x1, mod3, g_post, tiles_per_seq):
    n, d = x1.shape
    tm = TM_POST
    tok = lambda w: pl.BlockSpec((tm, w), lambda i: (i, 0))
    y_k = lambda k: pl.BlockSpec((tm,) + yt.shape[1:], lambda i: (k * (n // tm) + i, 0, 0))
    return pl.pallas_call(
        _combine_kernel,
        out_shape=jax.ShapeDtypeStruct((n, d), F32),
        grid=(n // tm,),
        in_specs=[y_k(k) for k in range(TOP_K)]
        + [tok(TOP_K), tok(d), pl.BlockSpec((1, 1, mod3.shape[2]), lambda i: (i // tiles_per_seq, 0, 0)),
           pl.BlockSpec(g_post.shape, lambda i: (0, 0))],
        out_specs=tok(d),
        compiler_params=_cparams(("arbitrary",)),
        name="combine",
    )(*([yt] * TOP_K), gates, x1, mod3, g_post)


def _layer(x, mod, positions, g_pre_mix, g_post_mix, g_pre_ffn, g_post_ffn, w_in, g_q_a, g_kv_a,
           w_q_b, w_kv_b, w_o, w_router, b_router, w_mlp1, b_mlp1, w_mlp2, b_mlp2):
    b, s, d = x.shape
    n = b * s
    mod3 = mod.reshape(b, 1, mod.shape[1])
    row = lambda g: g.reshape(1, -1)

    win, wq, wkv = _relayout_in_weights(w_in, w_q_b, w_kv_b)
    qm, km, vm, qd, kd, vd, qi, kiw = _inproj(
        x, mod3, positions.reshape(b, s, 1), row(g_pre_mix), win, row(g_q_a), row(g_kv_a), wq, wkv,
        _rope_rows())
    y_mla = _mla(qm, km, vm)
    y_dsa = _dsa(qi, kiw, qd, kd, vd, positions.reshape(b, s, 1),
                 positions.reshape(b, s // QB_DSA, 1, QB_DSA))

    wo = w_o.astype(BF16)
    split = MLA_HEADS * MLA_V
    x1, h2, meta, gates = _post(
        y_mla.reshape(n, -1), y_dsa.reshape(n, -1), x.reshape(n, d), mod3, wo[:split], wo[split:],
        row(g_post_mix), row(g_pre_ffn), w_router, row(b_router), s // TM_POST)

    n_slots = n * TOP_K + N_EXPERTS * BM_FFN
    dest, blk = _route(meta, n_slots // BM_FFN)
    dest_flat = dest.reshape(-1)
    xs, code, w1g, w1l = _dispatch(dest_flat, blk[1], blk[2, 0:1], h2, w_mlp1, n_slots)
    f = w_mlp2.shape[1]
    yt = _ffn(blk[0], blk[1], blk[2, 0:1], code, xs, w1g, w1l, w_mlp2,
              b_mlp1[:, 0::2].reshape(-1, 1, f), b_mlp1[:, 1::2].reshape(-1, 1, f),
              b_mlp2.reshape(-1, 1, d), n * TOP_K)
    out = _combine(yt, gates, x1, mod3, row(g_post_ffn), s // TM_POST)
    return out.reshape(b, s, d)


def kernel(x, c, positions, w_ada, b_ada, g_pre_mix, g_post_mix, g_pre_ffn, g_post_ffn, w_in, g_q_a, g_kv_a, w_q_b, w_kv_b, w_o, w_router, b_router, w_mlp1, b_mlp1, w_mlp2, b_mlp2):
    for l in range(w_ada.shape[0]):
        mod = _ada(c, w_ada[l], b_ada[l])
        x = _layer(x, mod, positions, g_pre_mix[l], g_post_mix[l], g_pre_ffn[l], g_post_ffn[l],
                   w_in[l], g_q_a[l], g_kv_a[l], w_q_b[l], w_kv_b[l], w_o[l], w_router[l], b_router[l],
                   w_mlp1[l], b_mlp1[l], w_mlp2[l], b_mlp2[l])
    return x
```

```python
import functools

import jax
import jax.numpy as jnp
from jax import lax
from jax.experimental import pallas as pl
from jax.experimental.pallas import tpu as pltpu

F32 = jnp.float32
BF16 = jnp.bfloat16
I32 = jnp.int32

MLA_HEADS = 8
MLA_Q_RANK = 256
MLA_KV_RANK = 128
MLA_NOPE = 64
MLA_ROPE = 32
MLA_V = 64
ROPE_THETA = 10000.0
DSA_HEADS = 8
DSA_KV_HEADS = 2
DSA_HEAD_DIM = 64
IDX_HEADS = 8
IDX_DIM = 32
IDX_TOPK_MAX = 256
N_EXPERTS = 32
TOP_K = 4
SWIGLU_ALPHA = 1.702
SWIGLU_LIMIT = 7.0
NORM_EPS = 1e-6
NEG_INF = -1e30
LOG2E = 1.4426950408889634

LANES = 128
HEAD_PAD = 128
VMEM_LIMIT = 52 * 1024 * 1024

TM_PROJ = 512
KV_CHUNK = 256
TQ_MLA = 512
QB_DSA = 128
TM_POST = 512
BM_FFN = 256
TM_ROWS = 256

EMPTY_SLOT = -1

_NT = (((1,), (1,)), ((), ()))


def _rms(x, g):
    ms = jnp.mean(x * x, axis=-1, keepdims=True)
    return x * lax.rsqrt(ms + NORM_EPS) * g


def _cparams(sem, vmem=VMEM_LIMIT):
    return pltpu.CompilerParams(dimension_semantics=sem, vmem_limit_bytes=vmem)


def _ada_kernel(c_ref, w_ref, b_ref, o_ref):
    c = c_ref[...]
    cond = c * jax.nn.sigmoid(c)
    o_ref[...] = jnp.dot(cond.astype(BF16), w_ref[...].astype(BF16),
                         preferred_element_type=F32) + b_ref[...]


def _ada(c, w_ada, b_ada):
    b, d = c.shape
    n = w_ada.shape[1]
    tn = 1024
    return pl.pallas_call(
        _ada_kernel,
        out_shape=jax.ShapeDtypeStruct((b, n), F32),
        grid=(n // tn,),
        in_specs=[pl.BlockSpec((b, d), lambda i: (0, 0)),
                  pl.BlockSpec((d, tn), lambda i: (0, i)),
                  pl.BlockSpec((1, tn), lambda i: (0, i))],
        out_specs=pl.BlockSpec((b, tn), lambda i: (0, i)),
        compiler_params=_cparams(("arbitrary",)),
        name="ada",
    )(c, w_ada, b_ada.reshape(1, n))


_O_CQ = 0
_O_CKV = _O_CQ + MLA_Q_RANK
_O_KRA = _O_CKV + MLA_KV_RANK
_O_KRB = _O_KRA + HEAD_PAD
_O_QD = _O_KRB + HEAD_PAD
_O_KD = _O_QD + DSA_HEADS * DSA_HEAD_DIM
_O_VD = _O_KD + DSA_KV_HEADS * DSA_HEAD_DIM
_O_QI = _O_VD + DSA_KV_HEADS * DSA_HEAD_DIM
_O_KIW = _O_QI + IDX_HEADS * IDX_DIM
_W_IN = _O_KIW + LANES


def _inproj_kernel(x_ref, mod_ref, pos_ref, gpre_ref, win_ref, gq_ref, gkv_ref, wq_ref, wkv_ref,
                   rope_ref, qm_ref, km_ref, vm_ref, qd_ref, kd_ref, vd_ref, qi_ref, kiw_ref):
    d = x_ref.shape[2]
    x = x_ref[0]
    sh1 = mod_ref[0, :, 0:d]
    sc1 = mod_ref[0, :, d:2 * d]
    h = _rms(x, gpre_ref[...]) * (1.0 + sc1) + sh1
    proj = jnp.dot(h.astype(BF16), win_ref[...], preferred_element_type=F32)

    ang = pos_ref[0].astype(F32) * rope_ref[0:1, :]
    cos_t = jnp.cos(ang) * rope_ref[1:2, :] + rope_ref[3:4, :]
    sin_t = jnp.sin(ang) * rope_ref[2:3, :]

    hw = MLA_HEADS * HEAD_PAD
    nq = _rms(proj[:, _O_CQ:_O_CQ + MLA_Q_RANK], gq_ref[...])
    qab = jnp.dot(nq.astype(BF16), wq_ref[...], preferred_element_type=F32)
    scale = (MLA_NOPE + MLA_ROPE) ** -0.5 * LOG2E
    nkv = _rms(proj[:, _O_CKV:_O_CKV + MLA_KV_RANK], gkv_ref[...])
    kv = jnp.dot(nkv.astype(BF16), wkv_ref[...], preferred_element_type=F32)
    kr = proj[:, _O_KRA:_O_KRA + HEAD_PAD] * cos_t + proj[:, _O_KRB:_O_KRB + HEAD_PAD] * sin_t
    for hd in range(MLA_HEADS):
        sl = slice(hd * HEAD_PAD, (hd + 1) * HEAD_PAD)
        slb = slice(hw + hd * HEAD_PAD, hw + (hd + 1) * HEAD_PAD)
        qm_ref[0, :, sl] = ((qab[:, sl] * cos_t + qab[:, slb] * sin_t) * scale).astype(BF16)
        km_ref[0, :, sl] = (kv[:, sl] + kr).astype(BF16)

    qd_ref[0] = (proj[:, _O_QD:_O_KD] * (DSA_HEAD_DIM ** -0.5 * LOG2E)).astype(BF16)
    kd_ref[0] = proj[:, _O_KD:_O_VD].astype(BF16)
    qi_ref[0] = (proj[:, _O_QI:_O_KIW] * (IDX_DIM ** -0.5)).astype(BF16)
    kiw_ref[0] = proj[:, _O_KIW:_W_IN]
    ck = vm_ref.shape[3]
    for t in range(x_ref.shape[1] // ck):
        rows = slice(t * ck, (t + 1) * ck)
        vm_ref[0, t] = kv[rows, hw:hw + MLA_HEADS * MLA_V].T.astype(BF16)
        vd_ref[0, t] = proj[rows, _O_VD:_O_QI].T.astype(BF16)


def _relayout_in_weights(w_in, w_q_b, w_kv_b):
    d = w_in.shape[0]
    half = MLA_ROPE // 2
    o = 0
    segs = {}
    for name, width in (("cq", MLA_Q_RANK), ("ckv", MLA_KV_RANK), ("kr", MLA_ROPE),
                        ("qd", DSA_HEADS * DSA_HEAD_DIM), ("kd", DSA_KV_HEADS * DSA_HEAD_DIM),
                        ("vd", DSA_KV_HEADS * DSA_HEAD_DIM), ("qi", IDX_HEADS * IDX_DIM),
                        ("ki", IDX_DIM), ("wi", IDX_HEADS)):
        segs[name] = w_in[:, o:o + width]
        o += width
    z = lambda n: jnp.zeros((d, n), w_in.dtype)
    x1, x2 = segs["kr"][:, :half], segs["kr"][:, half:]
    tail = HEAD_PAD - MLA_NOPE - MLA_ROPE
    kra = jnp.concatenate([z(MLA_NOPE), x1, x2, z(tail)], axis=1)
    krb = jnp.concatenate([z(MLA_NOPE), x2, x1, z(tail)], axis=1)
    win = jnp.concatenate([segs["cq"], segs["ckv"], kra, krb, segs["qd"], segs["kd"], segs["vd"],
                           segs["qi"], segs["ki"], segs["wi"], z(LANES - IDX_DIM - IDX_HEADS)], axis=1)
    r = w_q_b.shape[0]
    wq = w_q_b.reshape(r, MLA_HEADS, MLA_NOPE + MLA_ROPE)
    zq = lambda n: jnp.zeros((r, MLA_HEADS, n), w_q_b.dtype)
    wqa = jnp.concatenate([wq, zq(tail)], axis=2).reshape(r, MLA_HEADS * HEAD_PAD)
    wqb = jnp.concatenate([zq(MLA_NOPE), wq[:, :, MLA_NOPE + half:], wq[:, :, MLA_NOPE:MLA_NOPE + half],
                           zq(tail)], axis=2).reshape(r, MLA_HEADS * HEAD_PAD)
    rk = w_kv_b.shape[0]
    wkv = w_kv_b.reshape(rk, MLA_HEADS, MLA_NOPE + MLA_V)
    wkn = jnp.concatenate([wkv[:, :, :MLA_NOPE], jnp.zeros((rk, MLA_HEADS, HEAD_PAD - MLA_NOPE), w_kv_b.dtype)],
                          axis=2).reshape(rk, MLA_HEADS * HEAD_PAD)
    wv = wkv[:, :, MLA_NOPE:].reshape(rk, MLA_HEADS * MLA_V)
    return (win.astype(BF16), jnp.concatenate([wqa, wqb], axis=1).astype(BF16),
            jnp.concatenate([wkn, wv], axis=1).astype(BF16))


def _rope_rows():
    half = MLA_ROPE // 2
    lane = jnp.arange(LANES)
    freqs = ROPE_THETA ** (-jnp.arange(half, dtype=F32) / half)
    in_x1 = (lane >= MLA_NOPE) & (lane < MLA_NOPE + half)
    in_x2 = (lane >= MLA_NOPE + half) & (lane < MLA_NOPE + MLA_ROPE)
    fr = jnp.where(in_x1 | in_x2, freqs[(lane - MLA_NOPE) % half], 0.0)
    cosm = (in_x1 | in_x2).astype(F32)
    sinm = jnp.where(in_x1, -1.0, jnp.where(in_x2, 1.0, 0.0))
    nopem = (lane < MLA_NOPE).astype(F32)
    rows = jnp.stack([fr, cosm, sinm, nopem], axis=0).astype(F32)
    return jnp.concatenate([rows, jnp.zeros((4, LANES), F32)], axis=0)


def _inproj(x, mod3, pos3, g_pre, win, g_q, g_kv, wq, wkv, rope_rows):
    b, s, d = x.shape
    tm = TM_PROJ
    ck = KV_CHUNK
    tok = lambda w: pl.BlockSpec((1, tm, w), lambda bi, i: (bi, i, 0))
    full = lambda a: pl.BlockSpec(a.shape, lambda bi, i: (0,) * a.ndim)
    tr = lambda w: pl.BlockSpec((1, tm // ck, w, ck), lambda bi, i: (bi, i, 0, 0))
    tok_out = lambda w, dt: (jax.ShapeDtypeStruct((b, s, w), dt), tok(w))
    tr_out = lambda w: (jax.ShapeDtypeStruct((b, s // ck, w, ck), BF16), tr(w))
    outs = [tok_out(MLA_HEADS * HEAD_PAD, BF16), tok_out(MLA_HEADS * HEAD_PAD, BF16),
            tr_out(MLA_HEADS * MLA_V), tok_out(DSA_HEADS * DSA_HEAD_DIM, BF16),
            tok_out(DSA_KV_HEADS * DSA_HEAD_DIM, BF16), tr_out(DSA_KV_HEADS * DSA_HEAD_DIM),
            tok_out(IDX_HEADS * IDX_DIM, BF16), tok_out(LANES, F32)]
    return pl.pallas_call(
        _inproj_kernel,
        out_shape=[o[0] for o in outs],
        grid=(b, s // tm),
        in_specs=[tok(d),
                  pl.BlockSpec((1, 1, mod3.shape[2]), lambda bi, i: (bi, 0, 0)),
                  tok(1), full(g_pre), full(win), full(g_q), full(g_kv), full(wq), full(wkv),
                  full(rope_rows)],
        out_specs=[o[1] for o in outs],
        compiler_params=_cparams(("parallel", "arbitrary")),
        name="inproj",
    )(x, mod3, pos3, g_pre, win, g_q, g_kv, wq, wkv, rope_rows)


def _mla_kernel(q_ref, k_ref, vt_ref, o_ref, m_ref, l_ref, acc_ref):
    j = pl.program_id(1)
    tq = q_ref.shape[1]
    tk = k_ref.shape[1] // vt_ref.shape[1]
    m_ref[...] = jnp.full_like(m_ref, NEG_INF)
    l_ref[...] = jnp.zeros_like(l_ref)
    acc_ref[...] = jnp.zeros_like(acc_ref)
    per_q = tq // tk
    krow = lax.broadcasted_iota(I32, (tk, 1), 0)
    qcol = j * tq + lax.broadcasted_iota(I32, (1, tq), 1)

    def chunk(c, diagonal):
        off = pl.multiple_of(c * tk, tk)
        m_all = m_ref[...]
        l_all = l_ref[...]
        accs = [acc_ref[hd] for hd in range(MLA_HEADS)]
        ms, ls = [], []

        def scores(hd):
            lanes = slice(hd * HEAD_PAD, (hd + 1) * HEAD_PAD)
            return lax.dot_general(k_ref[0, pl.ds(off, tk), lanes], q_ref[0, :, lanes], _NT,
                                   preferred_element_type=F32)

        s_next = scores(0)
        pend = None
        for hd in range(MLA_HEADS):
            s = s_next
            if hd + 1 < MLA_HEADS:
                s_next = scores(hd + 1)
            if diagonal:
                s = jnp.where(off + krow <= qcol, s, NEG_INF)
            m_old = m_all[hd:hd + 1, :]
            m_new = jnp.maximum(m_old, jnp.max(s, axis=0, keepdims=True))
            alpha = jnp.exp2(m_old - m_new)
            p = jnp.exp2(s - m_new)
            ls.append(alpha * l_all[hd:hd + 1, :] + jnp.sum(p, axis=0, keepdims=True))
            ms.append(m_new)
            pv = jnp.dot(vt_ref[0, c, hd * MLA_V:(hd + 1) * MLA_V, :], p.astype(BF16),
                         preferred_element_type=F32)
            if pend is not None:
                accs[pend[0]] = pend[1] * accs[pend[0]] + pend[2]
            pend = (hd, alpha, pv)
        accs[pend[0]] = pend[1] * accs[pend[0]] + pend[2]
        m_ref[...] = jnp.concatenate(ms, axis=0)
        l_ref[...] = jnp.concatenate(ls, axis=0)
        for hd in range(MLA_HEADS):
            acc_ref[hd] = accs[hd]

    def body(c, carry):
        chunk(c, False)
        return carry

    lax.fori_loop(0, j * per_q, body, 0)
    for t in range(per_q):
        chunk(j * per_q + t, True)
    o_t = jnp.concatenate([acc_ref[hd] / l_ref[hd:hd + 1, :] for hd in range(MLA_HEADS)], axis=0)
    o_ref[0] = o_t.T.astype(BF16)


def _mla(qm, km, vmt):
    b, s, hw = qm.shape
    tq = TQ_MLA
    _, nck, vw, ck = vmt.shape
    return pl.pallas_call(
        _mla_kernel,
        out_shape=jax.ShapeDtypeStruct((b, s, vw), BF16),
        grid=(b, s // tq),
        in_specs=[pl.BlockSpec((1, tq, hw), lambda bi, j: (bi, j, 0)),
                  pl.BlockSpec((1, s, hw), lambda bi, j: (bi, 0, 0)),
                  pl.BlockSpec((1, nck, vw, ck), lambda bi, j: (bi, 0, 0, 0))],
        out_specs=pl.BlockSpec((1, tq, vw), lambda bi, j: (bi, j, 0)),
        scratch_shapes=[pltpu.VMEM((MLA_HEADS, tq), F32), pltpu.VMEM((MLA_HEADS, tq), F32),
                        pltpu.VMEM((MLA_HEADS, MLA_V, tq), F32)],
        compiler_params=_cparams(("parallel", "arbitrary")),
        name="mla",
    )(qm, km, vmt)


_KEY_NEG_INF = -2139095041
_KEY_POS_INF = 2139095040
_I32_MAX = 2147483647


def _key_to_f32(k):
    bits = k ^ ((k >> 31) & _I32_MAX)
    return lax.bitcast_convert_type(bits, F32)


def _dsa_kernel(qi_ref, kiw_ref, qd_ref, kd_ref, vdt_ref, posk_ref, posq_ref, y_ref, sc_ref, tie_ref, acc_ref):
    j = pl.program_id(1)
    qb, ck = QB_DSA, KV_CHUNK
    nch = (j * qb + qb + ck - 1) // ck
    n_sel = IDX_TOPK_MAX

    kiw_q = kiw_ref[0, pl.ds(pl.multiple_of(j * qb, qb), qb), :]
    w_t = kiw_q.T[IDX_DIM:IDX_DIM + IDX_HEADS, :] * (IDX_HEADS ** -0.5)
    qi = qi_ref[0]
    qi_stack = jnp.concatenate([qi[:, hd * IDX_DIM:(hd + 1) * IDX_DIM] for hd in range(IDX_HEADS)], axis=0)
    q_idx = j * qb + lax.broadcasted_iota(I32, (1, qb), 1)
    k_iota = lax.broadcasted_iota(I32, (ck, 1), 0)

    def idx_chunk(c, carry):
        off = pl.multiple_of(c * ck, ck)
        ki = kiw_ref[0, pl.ds(off, ck), :][:, 0:IDX_DIM].astype(BF16)
        r = lax.dot_general(ki, qi_stack, _NT, preferred_element_type=F32)
        acc = jnp.zeros((ck, qb), F32)
        for hd in range(IDX_HEADS):
            acc = acc + w_t[hd:hd + 1, :] * jnp.maximum(r[:, hd * qb:(hd + 1) * qb], 0.0)
        sc_ref[pl.ds(off, ck), :] = jnp.where(off + k_iota <= q_idx, acc, NEG_INF)
        return carry

    lax.fori_loop(0, nch, idx_chunk, 0)

    n_part = 4

    def count(pred_fn, ref=sc_ref):
        def body(c, parts):
            off = pl.multiple_of(c * ck, ck)
            hit = pred_fn(ref[pl.ds(off, ck), :], off).astype(F32)
            rows = ck // n_part
            return tuple(p + jnp.sum(hit[i * rows:(i + 1) * rows].reshape(rows // 8, 8, qb), axis=0)
                         for i, p in enumerate(parts))
        parts = lax.fori_loop(0, nch, body, (jnp.zeros((8, qb), F32),) * n_part)
        return jnp.sum((parts[0] + parts[1]) + (parts[2] + parts[3]), axis=0, keepdims=True)

    def no_tie(_):
        return jnp.full((1, qb), jnp.inf, F32), jnp.full((1, qb), _I32_MAX, I32)

    def index_cut(t_lo, t_hi):
        need = n_sel - count(lambda x, off: x >= t_hi)

        def mark(c, carry):
            off = pl.multiple_of(c * ck, ck)
            x = sc_ref[pl.ds(off, ck), :]
            tie_ref[pl.ds(off, ck), :] = ((x >= t_lo) & jnp.logical_not(x >= t_hi)).astype(F32)
            return carry

        lax.fori_loop(0, nch, mark, 0)

        def step(_, lohi):
            mlo, mhi = lohi
            mid = (mlo + mhi) >> 1
            cnt = count(lambda tie, off: jnp.where(off + k_iota <= mid, tie, 0.0), tie_ref)
            ok = cnt >= need
            return jnp.where(ok, mlo, mid), jnp.where(ok, mid, mhi)
        mlo0 = jnp.full((1, qb), -1, I32)
        mhi0 = jnp.full((1, qb), sc_ref.shape[0] - 1, I32)
        return lax.fori_loop(0, 12, step, (mlo0, mhi0))[1]

    def select(_):
        big = 3e38

        def reduce_rows(pred_fn):
            def body(c, carry):
                mn, mx = carry
                off = pl.multiple_of(c * ck, ck)
                x = sc_ref[pl.ds(off, ck), :]
                keep = pred_fn(x)
                mn = jnp.minimum(mn, jnp.min(jnp.where(keep, x, big).reshape(ck // 8, 8, qb), axis=0))
                mx = jnp.maximum(mx, jnp.max(jnp.where(keep, x, -big).reshape(ck // 8, 8, qb), axis=0))
                return mn, mx
            mn, mx = lax.fori_loop(0, nch, body, (jnp.full((8, qb), big, F32), jnp.full((8, qb), -big, F32)))
            return jnp.min(mn, axis=0, keepdims=True), jnp.max(mx, axis=0, keepdims=True)

        lo0, mx = reduce_rows(lambda x: x > 0.1 * NEG_INF)
        hi0 = mx + (jnp.abs(mx) + 1e-30) * 1e-6

        def bisect(_, st):
            lo, hi, c_lo, c_hi = st
            mid = lo + (hi - lo) * 0.5
            cnt = count(lambda x, off: x >= mid)
            ok = cnt >= n_sel
            return (jnp.where(ok, mid, lo), jnp.where(ok, hi, mid), jnp.where(ok, cnt, c_lo),
                    jnp.where(ok, c_hi, cnt))

        def round_(st):
            return (st[0] + 4,) + lax.fori_loop(0, 4, bisect, st[1:])

        def unsettled(st):
            return (st[0] < 24) & (jnp.max(jnp.abs(st[3] - n_sel)) > 0.5)

        _, lo, hi, c_lo, _ = lax.while_loop(
            unsettled, round_,
            (jnp.int32(0), lo0, hi0, (q_idx + 1).astype(F32), jnp.zeros((1, qb), F32)))
        open_ = jnp.abs(c_lo - n_sel) > 0.5

        def finish(_):
            bmin, bmax = reduce_rows(lambda x: (x >= lo) & jnp.logical_not(x >= hi))
            flat = jnp.max(jnp.where(open_, bmax - bmin, 0.0)) <= 0.0
            t_lo = jnp.where(open_, bmin, lo)
            t_hi = jnp.where(open_, hi, jnp.inf)
            return lax.cond(flat, lambda _: (t_lo, t_hi, index_cut(t_lo, t_hi)), select_exact, 0)

        return lax.cond(jnp.max(open_.astype(F32)) > 0.5, finish, lambda _: (lo,) + no_tie(0), 0)

    def select_exact(_):
        def bisect(_, st):
            lo, hi, c_lo = st
            mid = (lo & hi) + ((lo ^ hi) >> 1)
            t = _key_to_f32(mid)
            cnt = count(lambda x, off: x >= t)
            ok = cnt >= n_sel
            return jnp.where(ok, mid, lo), jnp.where(ok, hi, mid), jnp.where(ok, cnt, c_lo)

        per_round = 4

        def round_(st):
            it, lo, hi, c_lo = st
            lo, hi, c_lo = lax.fori_loop(0, per_round, bisect, (lo, hi, c_lo))
            return it + per_round, lo, hi, c_lo

        def unsettled(st):
            it, _, _, c_lo = st
            return (it < 32) & (jnp.max(jnp.abs(c_lo - n_sel)) > 0.5)

        st0 = (jnp.int32(0), jnp.full((1, qb), _KEY_NEG_INF, I32), jnp.full((1, qb), _KEY_POS_INF, I32),
               jnp.broadcast_to((nch * ck).astype(F32), (1, qb)))
        _, lo, _, c_lo = lax.while_loop(unsettled, round_, st0)
        t_lo = _key_to_f32(lo)

        def tie_search(_):
            t_hi = _key_to_f32(lo + 1)
            return t_hi, index_cut(t_lo, t_hi)

        tied = jnp.max(jnp.abs(c_lo - n_sel)) > 0.5
        t_hi, m_sel = lax.cond(tied, tie_search, no_tie, 0)
        return t_lo, t_hi, m_sel

    def all_causal(_):
        return (jnp.full((1, qb), 0.1 * NEG_INF, F32), jnp.full((1, qb), jnp.inf, F32),
                jnp.full((1, qb), _I32_MAX, I32))

    t_lo, t_hi, m_sel = lax.cond((j + 1) * qb > n_sel, select, all_causal, 0)

    grp = DSA_HEADS // DSA_KV_HEADS
    dh = DSA_HEAD_DIM
    qd = qd_ref[0]
    q_groups = [jnp.concatenate([qd[:, (g * grp + i) * dh:(g * grp + i + 1) * dh] for i in range(grp)], axis=0)
                for g in range(DSA_KV_HEADS)]
    posq = posq_ref[0, j]
    acc_ref[...] = jnp.zeros_like(acc_ref)

    def att_chunk(c, carry):
        ms, ls = carry
        off = pl.multiple_of(c * ck, ck)
        x = sc_ref[pl.ds(off, ck), :]
        sel = (x >= t_hi) | ((x >= t_lo) & (off + k_iota <= m_sel))
        dist = jnp.abs(posk_ref[0, pl.ds(off, ck), :] - posq).astype(F32)
        kd = kd_ref[0, pl.ds(off, ck), :]
        new_ms, new_ls = [], []
        s_all = [lax.dot_general(kd[:, g * dh:(g + 1) * dh], q_groups[g], _NT,
                                 preferred_element_type=F32) for g in range(DSA_KV_HEADS)]
        for g in range(DSA_KV_HEADS):
            s = s_all[g]
            ps, alphas = [], []
            for i in range(grp):
                hd = g * grp + i
                slope = 2.0 ** (-8.0 * (hd + 1) / DSA_HEADS) * LOG2E
                si = jnp.where(sel, s[:, i * qb:(i + 1) * qb] - slope * dist, NEG_INF)
                m_new = jnp.maximum(ms[hd], jnp.max(si, axis=0, keepdims=True))
                alpha = jnp.exp2(ms[hd] - m_new)
                p = jnp.exp2(si - m_new)
                new_ls.append(alpha * ls[hd] + jnp.sum(p, axis=0, keepdims=True))
                new_ms.append(m_new)
                ps.append(p.astype(BF16))
                alphas.append(alpha)
            pv = jnp.dot(vdt_ref[0, c, g * dh:(g + 1) * dh, :], jnp.concatenate(ps, axis=1),
                         preferred_element_type=F32)
            acc_ref[g] = jnp.concatenate(alphas, axis=1) * acc_ref[g] + pv
        return tuple(new_ms), tuple(new_ls)

    init = ((jnp.full((1, qb), NEG_INF, F32),) * DSA_HEADS, (jnp.zeros((1, qb), F32),) * DSA_HEADS)
    _, ls = lax.fori_loop(0, nch, att_chunk, init)
    o_t = jnp.concatenate([acc_ref[hd // grp][:, (hd % grp) * qb:(hd % grp + 1) * qb] / ls[hd]
                           for hd in range(DSA_HEADS)], axis=0)
    y_ref[0] = o_t.T.astype(BF16)


def _dsa(qi, kiw, qd, kd, vdt, posk, posq):
    b, s, _ = qd.shape
    qb = QB_DSA
    _, nck, vw, ck = vdt.shape
    grp = DSA_HEADS // DSA_KV_HEADS
    blk = lambda w: pl.BlockSpec((1, qb, w), lambda bi, j: (bi, j, 0))
    seq = lambda w: pl.BlockSpec((1, s, w), lambda bi, j: (bi, 0, 0))
    return pl.pallas_call(
        _dsa_kernel,
        out_shape=jax.ShapeDtypeStruct((b, s, qd.shape[2]), BF16),
        grid=(b, s // qb),
        in_specs=[blk(qi.shape[2]), seq(kiw.shape[2]), blk(qd.shape[2]), seq(kd.shape[2]),
                  pl.BlockSpec((1, nck, vw, ck), lambda bi, j: (bi, 0, 0, 0)),
                  seq(1),
                  pl.BlockSpec((1, s // qb, 1, qb), lambda bi, j: (bi, 0, 0, 0))],
        out_specs=blk(qd.shape[2]),
        scratch_shapes=[pltpu.VMEM((s, qb), F32), pltpu.VMEM((s, qb), F32),
                        pltpu.VMEM((DSA_KV_HEADS, DSA_HEAD_DIM, grp * qb), F32)],
        compiler_params=_cparams(("parallel", "arbitrary")),
        name="dsa",
    )(qi, kiw, qd, kd, vdt, posk, posq)


_META_ROWS = 16


def _post_kernel(ym_ref, yd_ref, x_ref, mod_ref, woa_ref, wob_ref, gpost_ref, gpre_ref, wr_ref, br_ref,
                 x1_ref, h2_ref, meta_ref, gates_ref, cnt_ref):
    i = pl.program_id(0)
    d = x_ref.shape[1]
    tm = x_ref.shape[0]

    @pl.when(i == 0)
    def _():
        cnt_ref[...] = jnp.zeros_like(cnt_ref)

    mix = (jnp.dot(ym_ref[...], woa_ref[...], preferred_element_type=F32)
           + jnp.dot(yd_ref[...], wob_ref[...], preferred_element_type=F32))
    gt1 = mod_ref[0, :, 2 * d:3 * d]
    sh2 = mod_ref[0, :, 3 * d:4 * d]
    sc2 = mod_ref[0, :, 4 * d:5 * d]
    x1 = x_ref[...] + gt1 * _rms(mix, gpost_ref[...])
    x1_ref[...] = x1
    h2 = _rms(x1, gpre_ref[...]) * (1.0 + sc2) + sh2
    h2_ref[...] = h2.reshape(h2_ref.shape)

    ne = br_ref.shape[1]
    h_hi = h2.astype(BF16)
    h_lo = (h2 - h_hi.astype(F32)).astype(BF16)
    hw = jnp.dot(h_hi, wr_ref[...], preferred_element_type=F32)
    logits = (hw[:, :ne] + hw[:, ne:] + jnp.dot(h_lo, wr_ref[:, :ne], preferred_element_type=F32)
              + br_ref[...])
    lane = lax.broadcasted_iota(I32, (tm, ne), 1).astype(F32)
    work = logits
    ids, vals = [], []
    for _ in range(TOP_K):
        mx = jnp.max(work, axis=-1, keepdims=True)
        idx = jnp.min(jnp.where(work == mx, lane, float(ne)), axis=-1, keepdims=True)
        ids.append(idx)
        vals.append(mx)
        work = jnp.where(lane == idx, -jnp.inf, work)
    es = [jnp.exp(v - vals[0]) for v in vals]
    den = es[0] + es[1] + es[2] + es[3]
    gates = [e / den for e in es]

    member = jnp.zeros((tm, ne), F32)
    for idx in ids:
        member = member + (lane == idx).astype(F32)
    r_i = lax.broadcasted_iota(I32, (tm, tm), 0)
    c_i = lax.broadcasted_iota(I32, (tm, tm), 1)
    tri = (c_i < r_i).astype(BF16)
    before = jnp.dot(tri, member.astype(BF16), preferred_element_type=F32) + cnt_ref[...]
    cnt_ref[...] = cnt_ref[...] + jnp.sum(member, axis=0, keepdims=True)
    ranks = [jnp.sum(jnp.where(lane == idx, before, 0.0), axis=-1, keepdims=True) for idx in ids]

    lane_m = lax.broadcasted_iota(I32, (tm, LANES), 1)
    rec = jnp.zeros((tm, LANES), F32)
    for k, col in enumerate(ids + gates + ranks):
        rec = jnp.where(lane_m == k, col, rec)
    meta_ref[...] = rec.T[0:_META_ROWS, :]
    lane_g = lax.broadcasted_iota(I32, (tm, TOP_K), 1)
    gm = jnp.zeros((tm, TOP_K), F32)
    for k, col in enumerate(gates):
        gm = jnp.where(lane_g == k, col, gm)
    gates_ref[...] = gm


def _post(ym, yd, x2, mod3, woa, wob, g_post, g_pre, w_router, b_router, tiles_per_seq):
    n, d = x2.shape
    tm = TM_POST
    ne = b_router.shape[1]
    w_hi = w_router.astype(BF16)
    w_router = jnp.concatenate([w_hi, (w_router - w_hi.astype(F32)).astype(BF16)], axis=1)
    tok = lambda w: pl.BlockSpec((tm, w), lambda i: (i, 0))
    full = lambda a: pl.BlockSpec(a.shape, lambda i: (0,) * a.ndim)
    return pl.pallas_call(
        _post_kernel,
        out_shape=[jax.ShapeDtypeStruct((n, d), F32), jax.ShapeDtypeStruct((n, d // LANES, LANES), F32),
                   jax.ShapeDtypeStruct((_META_ROWS, n), F32), jax.ShapeDtypeStruct((n, TOP_K), F32)],
        grid=(n // tm,),
        in_specs=[tok(ym.shape[1]), tok(yd.shape[1]), tok(d),
                  pl.BlockSpec((1, 1, mod3.shape[2]), lambda i: (i // tiles_per_seq, 0, 0)),
                  full(woa), full(wob), full(g_post), full(g_pre), full(w_router), full(b_router)],
        out_specs=[tok(d), pl.BlockSpec((tm, d // LANES, LANES), lambda i: (i, 0, 0)),
                   pl.BlockSpec((_META_ROWS, tm), lambda i: (0, i)), tok(TOP_K)],
        scratch_shapes=[pltpu.VMEM((1, ne), F32)],
        compiler_params=_cparams(("arbitrary",)),
        name="post",
    )(ym, yd, x2, mod3, woa, wob, g_post, g_pre, w_router, b_router)


def _route_kernel(meta_ref, dest_ref, blk_ref):
    ne = N_EXPERTS
    nbl = blk_ref.shape[1]
    e_col = lax.broadcasted_iota(I32, (ne, 1), 0).astype(F32)
    onehots = [(meta_ref[k:k + 1, :] == e_col) for k in range(TOP_K)]
    counts = jnp.zeros((ne, 1), F32)
    for oh in onehots:
        counts = counts + jnp.sum(oh.astype(F32), axis=1, keepdims=True)
    padded = jnp.floor((counts + float(BM_FFN - 1)) / BM_FFN) * BM_FFN
    r_i = lax.broadcasted_iota(I32, (ne, ne), 0)
    c_i = lax.broadcasted_iota(I32, (ne, ne), 1)
    tri = (c_i < r_i).astype(F32)
    padded_f = jnp.broadcast_to(padded, (ne, LANES))
    start = jnp.dot(tri, padded_f, preferred_element_type=F32, precision=lax.Precision.HIGHEST)[:, 0:1]
    for k in range(TOP_K):
        add = jnp.sum(jnp.where(onehots[k], start, 0.0), axis=0, keepdims=True)
        dest_ref[k:k + 1, :] = (meta_ref[2 * TOP_K + k:2 * TOP_K + k + 1, :] + add).astype(I32)
    end = start + padded
    row0 = (lax.broadcasted_iota(I32, (1, nbl), 1) * BM_FFN).astype(F32)
    blk_e = jnp.minimum(jnp.sum((end <= row0).astype(F32), axis=0, keepdims=True), float(ne - 1))
    own = blk_e == e_col
    real_end = jnp.sum(jnp.where(own, start + counts, 0.0), axis=0, keepdims=True)
    nvalid = jnp.clip(real_end - row0, 0.0, float(BM_FFN))
    nact = jnp.sum(padded, axis=0, keepdims=True) / BM_FFN
    rows = lax.broadcasted_iota(I32, blk_ref.shape, 0)
    out = jnp.where(rows == 0, blk_e, jnp.where(rows == 1, nvalid, jnp.broadcast_to(nact, blk_ref.shape)))
    blk_ref[...] = out.astype(I32)


def _route(meta, n_blocks):
    n = meta.shape[1]
    nbl = (n_blocks + LANES - 1) // LANES * LANES
    return pl.pallas_call(
        _route_kernel,
        out_shape=[jax.ShapeDtypeStruct((TOP_K, n), I32), jax.ShapeDtypeStruct((8, nbl), I32)],
        compiler_params=pltpu.CompilerParams(vmem_limit_bytes=VMEM_LIMIT),
        name="route",
    )(meta)


def _dispatch_kernel(dest_ref, nv_ref, na_ref, h_ref, w_ref, xs_ref, code_ref, g_ref, l_ref, zero_ref, t_ref,
                     sem, zsem):
    i = pl.program_id(0)
    tm = h_ref.shape[0]
    n = pl.num_programs(0) * tm
    bm = zero_ref.shape[0]

    @pl.when(i == 0)
    def _():
        zero_ref[...] = jnp.zeros_like(zero_ref)

        def fill(b, go):
            copy = pltpu.make_async_copy(zero_ref, xs_ref.at[pl.ds(b * bm, bm)], zsem)

            @pl.when((b >= na_ref[0]) | (nv_ref[b] < bm))
            def _():
                copy.start() if go else copy.wait()

        def mark_empty(b, c):
            first = jnp.where(b >= na_ref[0], 0, nv_ref[b])

            def mark(r, c2):
                code_ref[b * bm + r] = EMPTY_SLOT
                return c2
            return lax.fori_loop(first, bm, mark, c)

        n_blocks = xs_ref.shape[0] // bm
        lax.fori_loop(0, n_blocks, lambda b, c: (fill(b, True), c)[1], 0)
        lax.fori_loop(0, n_blocks, mark_empty, 0)
        lax.fori_loop(0, n_blocks, lambda b, c: (fill(b, False), c)[1], 0)

    def row_copy(t, k):
        dst = dest_ref[k * n + i * tm + t]
        return pltpu.make_async_copy(h_ref.at[t], xs_ref.at[dst], sem)

    for t in range(tm):
        for k in range(TOP_K):
            code_ref[dest_ref[k * n + i * tm + t]] = k * n + i * tm + t
            row_copy(t, k).start(priority=(t * TOP_K + k) % 2)
    _split_w1_block(w_ref, g_ref, l_ref, t_ref)
    for t in range(tm):
        for k in range(TOP_K):
            row_copy(t, k).wait()


def _split_w1_block(w_ref, g_ref, l_ref, t_ref):
    _, d, cw = w_ref.shape
    for r in range(d // LANES):
        rows = slice(r * LANES, (r + 1) * LANES)
        t_ref[r] = w_ref[0, rows, :].T
        g_ref[0, rows, :] = t_ref[r, pl.ds(0, cw // 2, stride=2), :].T.astype(BF16)
        l_ref[0, rows, :] = t_ref[r, pl.ds(1, cw // 2, stride=2), :].T.astype(BF16)


def _dispatch(dest_flat, blk_nv, n_act, h2, w1, n_slots):
    n, sub, ln = h2.shape
    e, d, f2 = w1.shape
    tm = TM_ROWS
    steps = n // tm
    per_e = steps // e
    cw = f2 // per_e
    assert per_e * e == steps and cw * per_e == f2 and cw % (2 * LANES) == 0
    w_out = jax.ShapeDtypeStruct((e, d, f2 // 2), BF16)
    slab = lambda width: pl.BlockSpec((1, d, width), lambda i, dest, nv, na: (i // per_e, 0, i % per_e))
    return pl.pallas_call(
        _dispatch_kernel,
        out_shape=[jax.ShapeDtypeStruct((n_slots, sub, ln), h2.dtype),
                   jax.ShapeDtypeStruct((n_slots,), I32), w_out, w_out],
        grid_spec=pltpu.PrefetchScalarGridSpec(
            num_scalar_prefetch=3,
            grid=(steps,),
            in_specs=[pl.BlockSpec((tm, sub, ln), lambda i, dest, nv, na: (i, 0, 0)), slab(cw)],
            out_specs=[pl.BlockSpec(memory_space=pl.ANY), pl.BlockSpec(memory_space=pltpu.SMEM),
                       slab(cw // 2), slab(cw // 2)],
            scratch_shapes=[pltpu.VMEM((BM_FFN, sub, ln), h2.dtype),
                            pltpu.VMEM((d // LANES, cw, LANES), F32),
                            pltpu.SemaphoreType.DMA, pltpu.SemaphoreType.DMA],
        ),
        compiler_params=_cparams(("arbitrary",)),
        name="dispatch",
    )(dest_flat, blk_nv, n_act, h2, w1)


def _ffn_kernel(be_ref, nv_ref, na_ref, code_ref, xs_ref, w1g_ref, w1l_ref, w2_ref, b1g_ref, b1l_ref, b2_ref,
                yt_ref, w2b_ref, stage_ref, sem):
    b = pl.program_id(0)
    steps = pl.num_programs(0)
    bm, sub, ln = xs_ref.shape
    n_rows = yt_ref.shape[0] - 2 * bm
    active = b < na_ref[0]

    def row_copy(step, slot, r, valid):
        code = jnp.where(valid, code_ref[step * bm + r], EMPTY_SLOT)
        dst = jnp.where(code == EMPTY_SLOT, n_rows + slot * bm + r, code)
        return pltpu.make_async_copy(stage_ref.at[slot, r], yt_ref.at[dst], sem.at[slot])

    def scatter(step, slot, go, valid=True):
        for r in range(bm):
            copy = row_copy(step, slot, r, valid)
            copy.start(priority=r % 2) if go else copy.wait()

    @pl.when(b == 0)
    def _():
        stage_ref[...] = jnp.zeros_like(stage_ref)
        spare = [pltpu.make_async_copy(stage_ref.at[p], yt_ref.at[pl.ds(n_rows + p * bm, bm)], sem.at[p])
                 for p in range(2)]
        for copy in spare:
            copy.start()
        for copy in spare:
            copy.wait()

    @pl.when(active & ((b == 0) | (be_ref[b] != be_ref[jnp.maximum(b - 1, 0)])))
    def _():
        w2b_ref[...] = w2_ref[0].astype(BF16)

    @pl.when(active)
    def _():
        prev = jnp.maximum(b - 1, 0)
        other = (b + 1) % 2
        scatter(prev, other, True, valid=b >= 1)
        rows = lax.broadcasted_iota(I32, (bm, 1), 0)
        x = jnp.where(rows < nv_ref[b], xs_ref[...].reshape(bm, sub * ln), 0.0).astype(BF16)
        glu = jnp.dot(x, w1g_ref[0], preferred_element_type=F32) + b1g_ref[0]
        lin = jnp.dot(x, w1l_ref[0], preferred_element_type=F32) + b1l_ref[0]
        glu = jnp.minimum(glu, SWIGLU_LIMIT)
        lin = jnp.clip(lin, -SWIGLU_LIMIT, SWIGLU_LIMIT)
        act = glu * jax.nn.sigmoid(SWIGLU_ALPHA * glu) * (lin + 1.0)
        y = jnp.dot(act.astype(BF16), w2b_ref[...], preferred_element_type=F32) + b2_ref[0]
        stage_ref[b % 2] = y.reshape(bm, sub, ln)
        scatter(prev, other, False, valid=b >= 1)

    @pl.when((b >= 1) & (b == na_ref[0]))
    def _():
        scatter(b - 1, (b - 1) % 2, True)
        scatter(b - 1, (b - 1) % 2, False)

    @pl.when(active & (b == steps - 1))
    def _():
        scatter(b, b % 2, True)
        scatter(b, b % 2, False)


def _ffn(blk_e, blk_nv, n_act, code, xs, w1g, w1l, w2, b1g, b1l, b2, n_rows):
    n_slots, sub, ln = xs.shape
    d = sub * ln
    bm = BM_FFN
    f = w1g.shape[2]
    last = lambda b, na: jnp.maximum(jnp.minimum(b, na[0] - 1), 0)
    blk = lambda b, be, nv, na, code: (last(b, na), 0, 0)
    exp3 = lambda b, be, nv, na, code: (be[last(b, na)], 0, 0)
    return pl.pallas_call(
        _ffn_kernel,
        out_shape=jax.ShapeDtypeStruct((n_rows + 2 * bm, sub, ln), F32),
        grid_spec=pltpu.PrefetchScalarGridSpec(
            num_scalar_prefetch=4,
            grid=(n_slots // bm,),
            in_specs=[pl.BlockSpec((bm, sub, ln), blk),
                      pl.BlockSpec((1, d, f), exp3), pl.BlockSpec((1, d, f), exp3),
                      pl.BlockSpec((1, f, d), exp3),
                      pl.BlockSpec((1, 1, f), exp3), pl.BlockSpec((1, 1, f), exp3),
                      pl.BlockSpec((1, 1, d), exp3)],
            out_specs=pl.BlockSpec(memory_space=pl.ANY),
            scratch_shapes=[pltpu.VMEM((f, d), BF16), pltpu.VMEM((2, bm, sub, ln), F32),
                            pltpu.SemaphoreType.DMA((2,))],
        ),
        compiler_params=_cparams(("arbitrary",)),
        name="ffn",
    )(blk_e, blk_nv, n_act, code, xs, w1g, w1l, w2, b1g, b1l, b2)


def _combine_kernel(*refs):
    y_refs, (gates_ref, x1_ref, mod_ref, gpost_ref, o_ref) = refs[:TOP_K], refs[TOP_K:]
    tm, d = x1_ref.shape
    g = gates_ref[...]
    ffn = g[:, 0:1] * y_refs[0][...].reshape(tm, d)
    for k in range(1, TOP_K):
        ffn = ffn + g[:, k:k + 1] * y_refs[k][...].reshape(tm, d)
    gt2 = mod_ref[0, :, 5 * d:6 * d]
    o_ref[...] = x1_ref[...] + gt2 * _rms(ffn, gpost_ref[...])


def _combine(yt, gates, x1, mod3, g_post, tiles_per_seq):
    n, d = x1.shape
    tm = TM_POST
    tok = lambda w: pl.BlockSpec((tm, w), lambda i: (i, 0))
    y_k = lambda k: pl.BlockSpec((tm,) + yt.shape[1:], lambda i: (k * (n // tm) + i, 0, 0))
    return pl.pallas_call(
        _combine_kernel,
        out_shape=jax.ShapeDtypeStruct((n, d), F32),
        grid=(n // tm,),
        in_specs=[y_k(k) for k in range(TOP_K)]
        + [tok(TOP_K), tok(d), pl.BlockSpec((1, 1, mod3.shape[2]), lambda i: (i // tiles_per_seq, 0, 0)),
           pl.BlockSpec(g_post.shape, lambda i: (0, 0))],
        out_specs=tok(d),
        compiler_params=_cparams(("arbitrary",)),
        name="combine",
    )(*([yt] * TOP_K), gates, x1, mod3, g_post)


def _layer(x, mod, positions, g_pre_mix, g_post_mix, g_pre_ffn, g_post_ffn, w_in, g_q_a, g_kv_a,
           w_q_b, w_kv_b, w_o, w_router, b_router, w_mlp1, b_mlp1, w_mlp2, b_mlp2):
    b, s, d = x.shape
    n = b * s
    mod3 = mod.reshape(b, 1, mod.shape[1])
    row = lambda g: g.reshape(1, -1)

    win, wq, wkv = _relayout_in_weights(w_in, w_q_b, w_kv_b)
    qm, km, vm, qd, kd, vd, qi, kiw = _inproj(
        x, mod3, positions.reshape(b, s, 1), row(g_pre_mix), win, row(g_q_a), row(g_kv_a), wq, wkv,
        _rope_rows())
    y_mla = _mla(qm, km, vm)
    y_dsa = _dsa(qi, kiw, qd, kd, vd, positions.reshape(b, s, 1),
                 positions.reshape(b, s // QB_DSA, 1, QB_DSA))

    wo = w_o.astype(BF16)
    split = MLA_HEADS * MLA_V
    x1, h2, meta, gates = _post(
        y_mla.reshape(n, -1), y_dsa.reshape(n, -1), x.reshape(n, d), mod3, wo[:split], wo[split:],
        row(g_post_mix), row(g_pre_ffn), w_router, row(b_router), s // TM_POST)

    n_slots = n * TOP_K + N_EXPERTS * BM_FFN
    dest, blk = _route(meta, n_slots // BM_FFN)
    dest_flat = dest.reshape(-1)
    xs, code, w1g, w1l = _dispatch(dest_flat, blk[1], blk[2, 0:1], h2, w_mlp1, n_slots)
    f = w_mlp2.shape[1]
    yt = _ffn(blk[0], blk[1], blk[2, 0:1], code, xs, w1g, w1l, w_mlp2,
              b_mlp1[:, 0::2].reshape(-1, 1, f), b_mlp1[:, 1::2].reshape(-1, 1, f),
              b_mlp2.reshape(-1, 1, d), n * TOP_K)
    out = _combine(yt, gates, x1, mod3, row(g_post_ffn), s // TM_POST)
    return out.reshape(b, s, d)


def kernel(x, c, positions, w_ada, b_ada, g_pre_mix, g_post_mix, g_pre_ffn, g_post_ffn, w_in, g_q_a, g_kv_a, w_q_b, w_kv_b, w_o, w_router, b_router, w_mlp1, b_mlp1, w_mlp2, b_mlp2):
    for l in range(w_ada.shape[0]):
        mod = _ada(c, w_ada[l], b_ada[l])
        x = _layer(x, mod, positions, g_pre_mix[l], g_post_mix[l], g_pre_ffn[l], g_post_ffn[l],
                   w_in[l], g_q_a[l], g_kv_a[l], w_q_b[l], w_kv_b[l], w_o[l], w_router[l], b_router[l],
                   w_mlp1[l], b_mlp1[l], w_mlp2[l], b_mlp2[l])
    return x
```

```python
import functools

import jax
import jax.numpy as jnp
from jax import lax
from jax.experimental import pallas as pl
from jax.experimental.pallas import tpu as pltpu

F32 = jnp.float32
BF16 = jnp.bfloat16
I32 = jnp.int32

MLA_HEADS = 8
MLA_Q_RANK = 256
MLA_KV_RANK = 128
MLA_NOPE = 64
MLA_ROPE = 32
MLA_V = 64
ROPE_THETA = 10000.0
DSA_HEADS = 8
DSA_KV_HEADS = 2
DSA_HEAD_DIM = 64
IDX_HEADS = 8
IDX_DIM = 32
IDX_TOPK_MAX = 256
N_EXPERTS = 32
TOP_K = 4
SWIGLU_ALPHA = 1.702
SWIGLU_LIMIT = 7.0
NORM_EPS = 1e-6
NEG_INF = -1e30
LOG2E = 1.4426950408889634

LANES = 128
HEAD_PAD = 128
VMEM_LIMIT = 52 * 1024 * 1024

TM_PROJ = 512
KV_CHUNK = 256
TQ_MLA = 512
QB_DSA = 128
TM_POST = 512
BM_FFN = 256
TM_ROWS = 256

EMPTY_SLOT = -1

_NT = (((1,), (1,)), ((), ()))


def _rms(x, g):
    ms = jnp.mean(x * x, axis=-1, keepdims=True)
    return x * lax.rsqrt(ms + NORM_EPS) * g


def _cparams(sem, vmem=VMEM_LIMIT):
    return pltpu.CompilerParams(dimension_semantics=sem, vmem_limit_bytes=vmem)


def _ada_kernel(c_ref, w_ref, b_ref, o_ref):
    c = c_ref[...]
    cond = c * jax.nn.sigmoid(c)
    o_ref[...] = jnp.dot(cond.astype(BF16), w_ref[...].astype(BF16),
                         preferred_element_type=F32) + b_ref[...]


def _ada(c, w_ada, b_ada):
    b, d = c.shape
    n = w_ada.shape[1]
    tn = 1024
    return pl.pallas_call(
        _ada_kernel,
        out_shape=jax.ShapeDtypeStruct((b, n), F32),
        grid=(n // tn,),
        in_specs=[pl.BlockSpec((b, d), lambda i: (0, 0)),
                  pl.BlockSpec((d, tn), lambda i: (0, i)),
                  pl.BlockSpec((1, tn), lambda i: (0, i))],
        out_specs=pl.BlockSpec((b, tn), lambda i: (0, i)),
        compiler_params=_cparams(("arbitrary",)),
        name="ada",
    )(c, w_ada, b_ada.reshape(1, n))


_O_CQ = 0
_O_CKV = _O_CQ + MLA_Q_RANK
_O_KRA = _O_CKV + MLA_KV_RANK
_O_KRB = _O_KRA + HEAD_PAD
_O_QD = _O_KRB + HEAD_PAD
_O_KD = _O_QD + DSA_HEADS * DSA_HEAD_DIM
_O_VD = _O_KD + DSA_KV_HEADS * DSA_HEAD_DIM
_O_QI = _O_VD + DSA_KV_HEADS * DSA_HEAD_DIM
_O_KIW = _O_QI + IDX_HEADS * IDX_DIM
_W_IN = _O_KIW + LANES


def _inproj_kernel(x_ref, mod_ref, pos_ref, gpre_ref, win_ref, gq_ref, gkv_ref, wq_ref, wkv_ref,
                   rope_ref, qm_ref, km_ref, vm_ref, qd_ref, kd_ref, vd_ref, qi_ref, kiw_ref):
    d = x_ref.shape[2]
    x = x_ref[0]
    sh1 = mod_ref[0, :, 0:d]
    sc1 = mod_ref[0, :, d:2 * d]
    h = _rms(x, gpre_ref[...]) * (1.0 + sc1) + sh1
    proj = jnp.dot(h.astype(BF16), win_ref[...], preferred_element_type=F32)

    ang = pos_ref[0].astype(F32) * rope_ref[0:1, :]
    cos_t = jnp.cos(ang) * rope_ref[1:2, :] + rope_ref[3:4, :]
    sin_t = jnp.sin(ang) * rope_ref[2:3, :]

    hw = MLA_HEADS * HEAD_PAD
    nq = _rms(proj[:, _O_CQ:_O_CQ + MLA_Q_RANK], gq_ref[...])
    qab = jnp.dot(nq.astype(BF16), wq_ref[...], preferred_element_type=F32)
    scale = (MLA_NOPE + MLA_ROPE) ** -0.5 * LOG2E
    nkv = _rms(proj[:, _O_CKV:_O_CKV + MLA_KV_RANK], gkv_ref[...])
    kv = jnp.dot(nkv.astype(BF16), wkv_ref[...], preferred_element_type=F32)
    kr = proj[:, _O_KRA:_O_KRA + HEAD_PAD] * cos_t + proj[:, _O_KRB:_O_KRB + HEAD_PAD] * sin_t
    for hd in range(MLA_HEADS):
        sl = slice(hd * HEAD_PAD, (hd + 1) * HEAD_PAD)
        slb = slice(hw + hd * HEAD_PAD, hw + (hd + 1) * HEAD_PAD)
        qm_ref[0, :, sl] = ((qab[:, sl] * cos_t + qab[:, slb] * sin_t) * scale).astype(BF16)
        km_ref[0, :, sl] = (kv[:, sl] + kr).astype(BF16)

    qd_ref[0] = (proj[:, _O_QD:_O_KD] * (DSA_HEAD_DIM ** -0.5 * LOG2E)).astype(BF16)
    kd_ref[0] = proj[:, _O_KD:_O_VD].astype(BF16)
    qi_ref[0] = (proj[:, _O_QI:_O_KIW] * (IDX_DIM ** -0.5)).astype(BF16)
    kiw_ref[0] = proj[:, _O_KIW:_W_IN]
    ck = vm_ref.shape[3]
    for t in range(x_ref.shape[1] // ck):
        rows = slice(t * ck, (t + 1) * ck)
        vm_ref[0, t] = kv[rows, hw:hw + MLA_HEADS * MLA_V].T.astype(BF16)
        vd_ref[0, t] = proj[rows, _O_VD:_O_QI].T.astype(BF16)


def _relayout_in_weights(w_in, w_q_b, w_kv_b):
    d = w_in.shape[0]
    half = MLA_ROPE // 2
    o = 0
    segs = {}
    for name, width in (("cq", MLA_Q_RANK), ("ckv", MLA_KV_RANK), ("kr", MLA_ROPE),
                        ("qd", DSA_HEADS * DSA_HEAD_DIM), ("kd", DSA_KV_HEADS * DSA_HEAD_DIM),
                        ("vd", DSA_KV_HEADS * DSA_HEAD_DIM), ("qi", IDX_HEADS * IDX_DIM),
                        ("ki", IDX_DIM), ("wi", IDX_HEADS)):
        segs[name] = w_in[:, o:o + width]
        o += width
    z = lambda n: jnp.zeros((d, n), w_in.dtype)
    x1, x2 = segs["kr"][:, :half], segs["kr"][:, half:]
    tail = HEAD_PAD - MLA_NOPE - MLA_ROPE
    kra = jnp.concatenate([z(MLA_NOPE), x1, x2, z(tail)], axis=1)
    krb = jnp.concatenate([z(MLA_NOPE), x2, x1, z(tail)], axis=1)
    win = jnp.concatenate([segs["cq"], segs["ckv"], kra, krb, segs["qd"], segs["kd"], segs["vd"],
                           segs["qi"], segs["ki"], segs["wi"], z(LANES - IDX_DIM - IDX_HEADS)], axis=1)
    r = w_q_b.shape[0]
    wq = w_q_b.reshape(r, MLA_HEADS, MLA_NOPE + MLA_ROPE)
    zq = lambda n: jnp.zeros((r, MLA_HEADS, n), w_q_b.dtype)
    wqa = jnp.concatenate([wq, zq(tail)], axis=2).reshape(r, MLA_HEADS * HEAD_PAD)
    wqb = jnp.concatenate([zq(MLA_NOPE), wq[:, :, MLA_NOPE + half:], wq[:, :, MLA_NOPE:MLA_NOPE + half],
                           zq(tail)], axis=2).reshape(r, MLA_HEADS * HEAD_PAD)
    rk = w_kv_b.shape[0]
    wkv = w_kv_b.reshape(rk, MLA_HEADS, MLA_NOPE + MLA_V)
    wkn = jnp.concatenate([wkv[:, :, :MLA_NOPE], jnp.zeros((rk, MLA_HEADS, HEAD_PAD - MLA_NOPE), w_kv_b.dtype)],
                          axis=2).reshape(rk, MLA_HEADS * HEAD_PAD)
    wv = wkv[:, :, MLA_NOPE:].reshape(rk, MLA_HEADS * MLA_V)
    return (win.astype(BF16), jnp.concatenate([wqa, wqb], axis=1).astype(BF16),
            jnp.concatenate([wkn, wv], axis=1).astype(BF16))


def _rope_rows():
    half = MLA_ROPE // 2
    lane = jnp.arange(LANES)
    freqs = ROPE_THETA ** (-jnp.arange(half, dtype=F32) / half)
    in_x1 = (lane >= MLA_NOPE) & (lane < MLA_NOPE + half)
    in_x2 = (lane >= MLA_NOPE + half) & (lane < MLA_NOPE + MLA_ROPE)
    fr = jnp.where(in_x1 | in_x2, freqs[(lane - MLA_NOPE) % half], 0.0)
    cosm = (in_x1 | in_x2).astype(F32)
    sinm = jnp.where(in_x1, -1.0, jnp.where(in_x2, 1.0, 0.0))
    nopem = (lane < MLA_NOPE).astype(F32)
    rows = jnp.stack([fr, cosm, sinm, nopem], axis=0).astype(F32)
    return jnp.concatenate([rows, jnp.zeros((4, LANES), F32)], axis=0)


def _inproj(x, mod3, pos3, g_pre, win, g_q, g_kv, wq, wkv, rope_rows):
    b, s, d = x.shape
    tm = TM_PROJ
    ck = KV_CHUNK
    tok = lambda w: pl.BlockSpec((1, tm, w), lambda bi, i: (bi, i, 0))
    full = lambda a: pl.BlockSpec(a.shape, lambda bi, i: (0,) * a.ndim)
    tr = lambda w: pl.BlockSpec((1, tm // ck, w, ck), lambda bi, i: (bi, i, 0, 0))
    tok_out = lambda w, dt: (jax.ShapeDtypeStruct((b, s, w), dt), tok(w))
    tr_out = lambda w: (jax.ShapeDtypeStruct((b, s // ck, w, ck), BF16), tr(w))
    outs = [tok_out(MLA_HEADS * HEAD_PAD, BF16), tok_out(MLA_HEADS * HEAD_PAD, BF16),
            tr_out(MLA_HEADS * MLA_V), tok_out(DSA_HEADS * DSA_HEAD_DIM, BF16),
            tok_out(DSA_KV_HEADS * DSA_HEAD_DIM, BF16), tr_out(DSA_KV_HEADS * DSA_HEAD_DIM),
            tok_out(IDX_HEADS * IDX_DIM, BF16), tok_out(LANES, F32)]
    return pl.pallas_call(
        _inproj_kernel,
        out_shape=[o[0] for o in outs],
        grid=(b, s // tm),
        in_specs=[tok(d),
                  pl.BlockSpec((1, 1, mod3.shape[2]), lambda bi, i: (bi, 0, 0)),
                  tok(1), full(g_pre), full(win), full(g_q), full(g_kv), full(wq), full(wkv),
                  full(rope_rows)],
        out_specs=[o[1] for o in outs],
        compiler_params=_cparams(("parallel", "arbitrary")),
        name="inproj",
    )(x, mod3, pos3, g_pre, win, g_q, g_kv, wq, wkv, rope_rows)


def _mla_kernel(q_ref, k_ref, vt_ref, o_ref, m_ref, l_ref, acc_ref):
    j = pl.program_id(1)
    tq = q_ref.shape[1]
    tk = k_ref.shape[1] // vt_ref.shape[1]
    m_ref[...] = jnp.full_like(m_ref, NEG_INF)
    l_ref[...] = jnp.zeros_like(l_ref)
    acc_ref[...] = jnp.zeros_like(acc_ref)
    per_q = tq // tk
    krow = lax.broadcasted_iota(I32, (tk, 1), 0)
    qcol = j * tq + lax.broadcasted_iota(I32, (1, tq), 1)

    def chunk(c, diagonal):
        off = pl.multiple_of(c * tk, tk)
        m_all = m_ref[...]
        l_all = l_ref[...]
        accs = [acc_ref[hd] for hd in range(MLA_HEADS)]
        ms, ls = [], []

        def scores(hd):
            lanes = slice(hd * HEAD_PAD, (hd + 1) * HEAD_PAD)
            return lax.dot_general(k_ref[0, pl.ds(off, tk), lanes], q_ref[0, :, lanes], _NT,
                                   preferred_element_type=F32)

        s_next = scores(0)
        pend = None
        for hd in range(MLA_HEADS):
            s = s_next
            if hd + 1 < MLA_HEADS:
                s_next = scores(hd + 1)
            if diagonal:
                s = jnp.where(off + krow <= qcol, s, NEG_INF)
            m_old = m_all[hd:hd + 1, :]
            m_new = jnp.maximum(m_old, jnp.max(s, axis=0, keepdims=True))
            alpha = jnp.exp2(m_old - m_new)
            p = jnp.exp2(s - m_new)
            ls.append(alpha * l_all[hd:hd + 1, :] + jnp.sum(p, axis=0, keepdims=True))
            ms.append(m_new)
            pv = jnp.dot(vt_ref[0, c, hd * MLA_V:(hd + 1) * MLA_V, :], p.astype(BF16),
                         preferred_element_type=F32)
            if pend is not None:
                accs[pend[0]] = pend[1] * accs[pend[0]] + pend[2]
            pend = (hd, alpha, pv)
        accs[pend[0]] = pend[1] * accs[pend[0]] + pend[2]
        m_ref[...] = jnp.concatenate(ms, axis=0)
        l_ref[...] = jnp.concatenate(ls, axis=0)
        for hd in range(MLA_HEADS):
            acc_ref[hd] = accs[hd]

    def body(c, carry):
        chunk(c, False)
        return carry

    lax.fori_loop(0, j * per_q, body, 0)
    for t in range(per_q):
        chunk(j * per_q + t, True)
    o_t = jnp.concatenate([acc_ref[hd] / l_ref[hd:hd + 1, :] for hd in range(MLA_HEADS)], axis=0)
    o_ref[0] = o_t.T.astype(BF16)


def _mla(qm, km, vmt):
    b, s, hw = qm.shape
    tq = TQ_MLA
    _, nck, vw, ck = vmt.shape
    return pl.pallas_call(
        _mla_kernel,
        out_shape=jax.ShapeDtypeStruct((b, s, vw), BF16),
        grid=(b, s // tq),
        in_specs=[pl.BlockSpec((1, tq, hw), lambda bi, j: (bi, j, 0)),
                  pl.BlockSpec((1, s, hw), lambda bi, j: (bi, 0, 0)),
                  pl.BlockSpec((1, nck, vw, ck), lambda bi, j: (bi, 0, 0, 0))],
        out_specs=pl.BlockSpec((1, tq, vw), lambda bi, j: (bi, j, 0)),
        scratch_shapes=[pltpu.VMEM((MLA_HEADS, tq), F32), pltpu.VMEM((MLA_HEADS, tq), F32),
                        pltpu.VMEM((MLA_HEADS, MLA_V, tq), F32)],
        compiler_params=_cparams(("parallel", "arbitrary")),
        name="mla",
    )(qm, km, vmt)


_KEY_NEG_INF = -2139095041
_KEY_POS_INF = 2139095040
_I32_MAX = 2147483647


def _key_to_f32(k):
    bits = k ^ ((k >> 31) & _I32_MAX)
    return lax.bitcast_convert_type(bits, F32)


def _dsa_kernel(qi_ref, kiw_ref, qd_ref, kd_ref, vdt_ref, posk_ref, posq_ref, y_ref, sc_ref, tie_ref, acc_ref):
    j = pl.program_id(1)
    qb, ck = QB_DSA, KV_CHUNK
    nch = (j * qb + qb + ck - 1) // ck
    n_sel = IDX_TOPK_MAX

    kiw_q = kiw_ref[0, pl.ds(pl.multiple_of(j * qb, qb), qb), :]
    w_t = kiw_q.T[IDX_DIM:IDX_DIM + IDX_HEADS, :] * (IDX_HEADS ** -0.5)
    qi = qi_ref[0]
    qi_stack = jnp.concatenate([qi[:, hd * IDX_DIM:(hd + 1) * IDX_DIM] for hd in range(IDX_HEADS)], axis=0)
    q_idx = j * qb + lax.broadcasted_iota(I32, (1, qb), 1)
    k_iota = lax.broadcasted_iota(I32, (ck, 1), 0)

    def idx_chunk(c, carry):
        off = pl.multiple_of(c * ck, ck)
        ki = kiw_ref[0, pl.ds(off, ck), :][:, 0:IDX_DIM].astype(BF16)
        r = lax.dot_general(ki, qi_stack, _NT, preferred_element_type=F32)
        acc = jnp.zeros((ck, qb), F32)
        for hd in range(IDX_HEADS):
            acc = acc + w_t[hd:hd + 1, :] * jnp.maximum(r[:, hd * qb:(hd + 1) * qb], 0.0)
        sc_ref[pl.ds(off, ck), :] = jnp.where(off + k_iota <= q_idx, acc, NEG_INF)
        return carry

    lax.fori_loop(0, nch, idx_chunk, 0)

    n_part = 4

    def count(pred_fn, ref=sc_ref):
        def body(c, parts):
            off = pl.multiple_of(c * ck, ck)
            hit = pred_fn(ref[pl.ds(off, ck), :], off).astype(F32)
            rows = ck // n_part
            return tuple(p + jnp.sum(hit[i * rows:(i + 1) * rows].reshape(rows // 8, 8, qb), axis=0)
                         for i, p in enumerate(parts))
        parts = lax.fori_loop(0, nch, body, (jnp.zeros((8, qb), F32),) * n_part)
        return jnp.sum((parts[0] + parts[1]) + (parts[2] + parts[3]), axis=0, keepdims=True)

    def no_tie(_):
        return jnp.full((1, qb), jnp.inf, F32), jnp.full((1, qb), _I32_MAX, I32)

    def index_cut(t_lo, t_hi):
        need = n_sel - count(lambda x, off: x >= t_hi)

        def mark(c, carry):
            off = pl.multiple_of(c * ck, ck)
            x = sc_ref[pl.ds(off, ck), :]
            tie_ref[pl.ds(off, ck), :] = ((x >= t_lo) & jnp.logical_not(x >= t_hi)).astype(F32)
            return carry

        lax.fori_loop(0, nch, mark, 0)

        def step(_, lohi):
            mlo, mhi = lohi
            mid = (mlo + mhi) >> 1
            cnt = count(lambda tie, off: jnp.where(off + k_iota <= mid, tie, 0.0), tie_ref)
            ok = cnt >= need
            return jnp.where(ok, mlo, mid), jnp.where(ok, mid, mhi)
        mlo0 = jnp.full((1, qb), -1, I32)
        mhi0 = jnp.full((1, qb), sc_ref.shape[0] - 1, I32)
        return lax.fori_loop(0, 12, step, (mlo0, mhi0))[1]

    def select(_):
        big = 3e38

        def reduce_rows(pred_fn):
            def body(c, carry):
                mn, mx = carry
                off = pl.multiple_of(c * ck, ck)
                x = sc_ref[pl.ds(off, ck), :]
                keep = pred_fn(x)
                mn = jnp.minimum(mn, jnp.min(jnp.where(keep, x, big).reshape(ck // 8, 8, qb), axis=0))
                mx = jnp.maximum(mx, jnp.max(jnp.where(keep, x, -big).reshape(ck // 8, 8, qb), axis=0))
                return mn, mx
            mn, mx = lax.fori_loop(0, nch, body, (jnp.full((8, qb), big, F32), jnp.full((8, qb), -big, F32)))
            return jnp.min(mn, axis=0, keepdims=True), jnp.max(mx, axis=0, keepdims=True)

        lo0, mx = reduce_rows(lambda x: x > 0.1 * NEG_INF)
        hi0 = mx + (jnp.abs(mx) + 1e-30) * 1e-6

        def bisect(_, st):
            lo, hi, c_lo, c_hi = st
            mid = lo + (hi - lo) * 0.5
            cnt = count(lambda x, off: x >= mid)
            ok = cnt >= n_sel
            return (jnp.where(ok, mid, lo), jnp.where(ok, hi, mid), jnp.where(ok, cnt, c_lo),
                    jnp.where(ok, c_hi, cnt))

        def round_(st):
            return (st[0] + 4,) + lax.fori_loop(0, 4, bisect, st[1:])

        def unsettled(st):
            return (st[0] < 24) & (jnp.max(jnp.abs(st[3] - n_sel)) > 0.5)

        _, lo, hi, c_lo, _ = lax.while_loop(
            unsettled, round_,
            (jnp.int32(0), lo0, hi0, (q_idx + 1).astype(F32), jnp.zeros((1, qb), F32)))
        open_ = jnp.abs(c_lo - n_sel) > 0.5

        def finish(_):
            bmin, bmax = reduce_rows(lambda x: (x >= lo) & jnp.logical_not(x >= hi))
            flat = jnp.max(jnp.where(open_, bmax - bmin, 0.0)) <= 0.0
            t_lo = jnp.where(open_, bmin, lo)
            t_hi = jnp.where(open_, hi, jnp.inf)
            return lax.cond(flat, lambda _: (t_lo, t_hi, index_cut(t_lo, t_hi)), select_exact, 0)

        return lax.cond(jnp.max(open_.astype(F32)) > 0.5, finish, lambda _: (lo,) + no_tie(0), 0)

    def select_exact(_):
        def bisect(_, st):
            lo, hi, c_lo = st
            mid = (lo & hi) + ((lo ^ hi) >> 1)
            t = _key_to_f32(mid)
            cnt = count(lambda x, off: x >= t)
            ok = cnt >= n_sel
            return jnp.where(ok, mid, lo), jnp.where(ok, hi, mid), jnp.where(ok, cnt, c_lo)

        per_round = 4

        def round_(st):
            it, lo, hi, c_lo = st
            lo, hi, c_lo = lax.fori_loop(0, per_round, bisect, (lo, hi, c_lo))
            return it + per_round, lo, hi, c_lo

        def unsettled(st):
            it, _, _, c_lo = st
            return (it < 32) & (jnp.max(jnp.abs(c_lo - n_sel)) > 0.5)

        st0 = (jnp.int32(0), jnp.full((1, qb), _KEY_NEG_INF, I32), jnp.full((1, qb), _KEY_POS_INF, I32),
               jnp.broadcast_to((nch * ck).astype(F32), (1, qb)))
        _, lo, _, c_lo = lax.while_loop(unsettled, round_, st0)
        t_lo = _key_to_f32(lo)

        def tie_search(_):
            t_hi = _key_to_f32(lo + 1)
            return t_hi, index_cut(t_lo, t_hi)

        tied = jnp.max(jnp.abs(c_lo - n_sel)) > 0.5
        t_hi, m_sel = lax.cond(tied, tie_search, no_tie, 0)
        return t_lo, t_hi, m_sel

    def all_causal(_):
        return (jnp.full((1, qb), 0.1 * NEG_INF, F32), jnp.full((1, qb), jnp.inf, F32),
                jnp.full((1, qb), _I32_MAX, I32))

    t_lo, t_hi, m_sel = lax.cond((j + 1) * qb > n_sel, select, all_causal, 0)

    grp = DSA_HEADS // DSA_KV_HEADS
    dh = DSA_HEAD_DIM
    qd = qd_ref[0]
    q_groups = [jnp.concatenate([qd[:, (g * grp + i) * dh:(g * grp + i + 1) * dh] for i in range(grp)], axis=0)
                for g in range(DSA_KV_HEADS)]
    posq = posq_ref[0, j]
    acc_ref[...] = jnp.zeros_like(acc_ref)

    def att_chunk(c, carry):
        ms, ls = carry
        off = pl.multiple_of(c * ck, ck)
        x = sc_ref[pl.ds(off, ck), :]
        sel = (x >= t_hi) | ((x >= t_lo) & (off + k_iota <= m_sel))
        dist = jnp.abs(posk_ref[0, pl.ds(off, ck), :] - posq).astype(F32)
        kd = kd_ref[0, pl.ds(off, ck), :]
        new_ms, new_ls = [], []
        s_all = [lax.dot_general(kd[:, g * dh:(g + 1) * dh], q_groups[g], _NT,
                                 preferred_element_type=F32) for g in range(DSA_KV_HEADS)]
        for g in range(DSA_KV_HEADS):
            s = s_all[g]
            ps, alphas = [], []
            for i in range(grp):
                hd = g * grp + i
                slope = 2.0 ** (-8.0 * (hd + 1) / DSA_HEADS) * LOG2E
                si = jnp.where(sel, s[:, i * qb:(i + 1) * qb] - slope * dist, NEG_INF)
                m_new = jnp.maximum(ms[hd], jnp.max(si, axis=0, keepdims=True))
                alpha = jnp.exp2(ms[hd] - m_new)
                p = jnp.exp2(si - m_new)
                new_ls.append(alpha * ls[hd] + jnp.sum(p, axis=0, keepdims=True))
                new_ms.append(m_new)
                ps.append(p.astype(BF16))
                alphas.append(alpha)
            pv = jnp.dot(vdt_ref[0, c, g * dh:(g + 1) * dh, :], jnp.concatenate(ps, axis=1),
                         preferred_element_type=F32)
            acc_ref[g] = jnp.concatenate(alphas, axis=1) * acc_ref[g] + pv
        return tuple(new_ms), tuple(new_ls)

    init = ((jnp.full((1, qb), NEG_INF, F32),) * DSA_HEADS, (jnp.zeros((1, qb), F32),) * DSA_HEADS)
    _, ls = lax.fori_loop(0, nch, att_chunk, init)
    o_t = jnp.concatenate([acc_ref[hd // grp][:, (hd % grp) * qb:(hd % grp + 1) * qb] / ls[hd]
                           for hd in range(DSA_HEADS)], axis=0)
    y_ref[0] = o_t.T.astype(BF16)


def _dsa(qi, kiw, qd, kd, vdt, posk, posq):
    b, s, _ = qd.shape
    qb = QB_DSA
    _, nck, vw, ck = vdt.shape
    grp = DSA_HEADS // DSA_KV_HEADS
    blk = lambda w: pl.BlockSpec((1, qb, w), lambda bi, j: (bi, j, 0))
    seq = lambda w: pl.BlockSpec((1, s, w), lambda bi, j: (bi, 0, 0))
    return pl.pallas_call(
        _dsa_kernel,
        out_shape=jax.ShapeDtypeStruct((b, s, qd.shape[2]), BF16),
        grid=(b, s // qb),
        in_specs=[blk(qi.shape[2]), seq(kiw.shape[2]), blk(qd.shape[2]), seq(kd.shape[2]),
                  pl.BlockSpec((1, nck, vw, ck), lambda bi, j: (bi, 0, 0, 0)),
                  seq(1),
                  pl.BlockSpec((1, s // qb, 1, qb), lambda bi, j: (bi, 0, 0, 0))],
        out_specs=blk(qd.shape[2]),
        scratch_shapes=[pltpu.VMEM((s, qb), F32), pltpu.VMEM((s, qb), F32),
                        pltpu.VMEM((DSA_KV_HEADS, DSA_HEAD_DIM, grp * qb), F32)],
        compiler_params=_cparams(("parallel", "arbitrary")),
        name="dsa",
    )(qi, kiw, qd, kd, vdt, posk, posq)


_META_ROWS = 16


def _post_kernel(ym_ref, yd_ref, x_ref, mod_ref, woa_ref, wob_ref, gpost_ref, gpre_ref, wr_ref, br_ref,
                 x1_ref, h2_ref, meta_ref, gates_ref, cnt_ref):
    i = pl.program_id(0)
    d = x_ref.shape[1]
    tm = x_ref.shape[0]

    @pl.when(i == 0)
    def _():
        cnt_ref[...] = jnp.zeros_like(cnt_ref)

    mix = (jnp.dot(ym_ref[...], woa_ref[...], preferred_element_type=F32)
           + jnp.dot(yd_ref[...], wob_ref[...], preferred_element_type=F32))
    gt1 = mod_ref[0, :, 2 * d:3 * d]
    sh2 = mod_ref[0, :, 3 * d:4 * d]
    sc2 = mod_ref[0, :, 4 * d:5 * d]
    x1 = x_ref[...] + gt1 * _rms(mix, gpost_ref[...])
    x1_ref[...] = x1
    h2 = _rms(x1, gpre_ref[...]) * (1.0 + sc2) + sh2
    h2_ref[...] = h2.reshape(h2_ref.shape)

    ne = br_ref.shape[1]
    h_hi = h2.astype(BF16)
    h_lo = (h2 - h_hi.astype(F32)).astype(BF16)
    hw = jnp.dot(h_hi, wr_ref[...], preferred_element_type=F32)
    logits = (hw[:, :ne] + hw[:, ne:] + jnp.dot(h_lo, wr_ref[:, :ne], preferred_element_type=F32)
              + br_ref[...])
    lane = lax.broadcasted_iota(I32, (tm, ne), 1).astype(F32)
    work = logits
    ids, vals = [], []
    for _ in range(TOP_K):
        mx = jnp.max(work, axis=-1, keepdims=True)
        idx = jnp.min(jnp.where(work == mx, lane, float(ne)), axis=-1, keepdims=True)
        ids.append(idx)
        vals.append(mx)
        work = jnp.where(lane == idx, -jnp.inf, work)
    es = [jnp.exp(v - vals[0]) for v in vals]
    den = es[0] + es[1] + es[2] + es[3]
    gates = [e / den for e in es]

    member = jnp.zeros((tm, ne), F32)
    for idx in ids:
        member = member + (lane == idx).astype(F32)
    r_i = lax.broadcasted_iota(I32, (tm, tm), 0)
    c_i = lax.broadcasted_iota(I32, (tm, tm), 1)
    tri = (c_i < r_i).astype(BF16)
    before = jnp.dot(tri, member.astype(BF16), preferred_element_type=F32) + cnt_ref[...]
    cnt_ref[...] = cnt_ref[...] + jnp.sum(member, axis=0, keepdims=True)
    ranks = [jnp.sum(jnp.where(lane == idx, before, 0.0), axis=-1, keepdims=True) for idx in ids]

    lane_m = lax.broadcasted_iota(I32, (tm, LANES), 1)
    rec = jnp.zeros((tm, LANES), F32)
    for k, col in enumerate(ids + gates + ranks):
        rec = jnp.where(lane_m == k, col, rec)
    meta_ref[...] = rec.T[0:_META_ROWS, :]
    lane_g = lax.broadcasted_iota(I32, (tm, TOP_K), 1)
    gm = jnp.zeros((tm, TOP_K), F32)
    for k, col in enumerate(gates):
        gm = jnp.where(lane_g == k, col, gm)
    gates_ref[...] = gm


def _post(ym, yd, x2, mod3, woa, wob, g_post, g_pre, w_router, b_router, tiles_per_seq):
    n, d = x2.shape
    tm = TM_POST
    ne = b_router.shape[1]
    w_hi = w_router.astype(BF16)
    w_router = jnp.concatenate([w_hi, (w_router - w_hi.astype(F32)).astype(BF16)], axis=1)
    tok = lambda w: pl.BlockSpec((tm, w), lambda i: (i, 0))
    full = lambda a: pl.BlockSpec(a.shape, lambda i: (0,) * a.ndim)
    return pl.pallas_call(
        _post_kernel,
        out_shape=[jax.ShapeDtypeStruct((n, d), F32), jax.ShapeDtypeStruct((n, d // LANES, LANES), F32),
                   jax.ShapeDtypeStruct((_META_ROWS, n), F32), jax.ShapeDtypeStruct((n, TOP_K), F32)],
        grid=(n // tm,),
        in_specs=[tok(ym.shape[1]), tok(yd.shape[1]), tok(d),
                  pl.BlockSpec((1, 1, mod3.shape[2]), lambda i: (i // tiles_per_seq, 0, 0)),
                  full(woa), full(wob), full(g_post), full(g_pre), full(w_router), full(b_router)],
        out_specs=[tok(d), pl.BlockSpec((tm, d // LANES, LANES), lambda i: (i, 0, 0)),
                   pl.BlockSpec((_META_ROWS, tm), lambda i: (0, i)), tok(TOP_K)],
        scratch_shapes=[pltpu.VMEM((1, ne), F32)],
        compiler_params=_cparams(("arbitrary",)),
        name="post",
    )(ym, yd, x2, mod3, woa, wob, g_post, g_pre, w_router, b_router)


def _route_kernel(meta_ref, dest_ref, blk_ref):
    ne = N_EXPERTS
    nbl = blk_ref.shape[1]
    e_col = lax.broadcasted_iota(I32, (ne, 1), 0).astype(F32)
    onehots = [(meta_ref[k:k + 1, :] == e_col) for k in range(TOP_K)]
    counts = jnp.zeros((ne, 1), F32)
    for oh in onehots:
        counts = counts + jnp.sum(oh.astype(F32), axis=1, keepdims=True)
    padded = jnp.floor((counts + float(BM_FFN - 1)) / BM_FFN) * BM_FFN
    r_i = lax.broadcasted_iota(I32, (ne, ne), 0)
    c_i = lax.broadcasted_iota(I32, (ne, ne), 1)
    tri = (c_i < r_i).astype(F32)
    padded_f = jnp.broadcast_to(padded, (ne, LANES))
    start = jnp.dot(tri, padded_f, preferred_element_type=F32, precision=lax.Precision.HIGHEST)[:, 0:1]
    for k in range(TOP_K):
        add = jnp.sum(jnp.where(onehots[k], start, 0.0), axis=0, keepdims=True)
        dest_ref[k:k + 1, :] = (meta_ref[2 * TOP_K + k:2 * TOP_K + k + 1, :] + add).astype(I32)
    end = start + padded
    row0 = (lax.broadcasted_iota(I32, (1, nbl), 1) * BM_FFN).astype(F32)
    blk_e = jnp.minimum(jnp.sum((end <= row0).astype(F32), axis=0, keepdims=True), float(ne - 1))
    own = blk_e == e_col
    real_end = jnp.sum(jnp.where(own, start + counts, 0.0), axis=0, keepdims=True)
    nvalid = jnp.clip(real_end - row0, 0.0, float(BM_FFN))
    nact = jnp.sum(padded, axis=0, keepdims=True) / BM_FFN
    rows = lax.broadcasted_iota(I32, blk_ref.shape, 0)
    out = jnp.where(rows == 0, blk_e, jnp.where(rows == 1, nvalid, jnp.broadcast_to(nact, blk_ref.shape)))
    blk_ref[...] = out.astype(I32)


def _route(meta, n_blocks):
    n = meta.shape[1]
    nbl = (n_blocks + LANES - 1) // LANES * LANES
    return pl.pallas_call(
        _route_kernel,
        out_shape=[jax.ShapeDtypeStruct((TOP_K, n), I32), jax.ShapeDtypeStruct((8, nbl), I32)],
        compiler_params=pltpu.CompilerParams(vmem_limit_bytes=VMEM_LIMIT),
        name="route",
    )(meta)


def _dispatch_kernel(dest_ref, nv_ref, na_ref, h_ref, w_ref, xs_ref, code_ref, g_ref, l_ref, zero_ref, t_ref,
                     sem, zsem):
    i = pl.program_id(0)
    tm = h_ref.shape[0]
    n = pl.num_programs(0) * tm
    bm = zero_ref.shape[0]

    @pl.when(i == 0)
    def _():
        zero_ref[...] = jnp.zeros_like(zero_ref)

        def fill(b, go):
            copy = pltpu.make_async_copy(zero_ref, xs_ref.at[pl.ds(b * bm, bm)], zsem)

            @pl.when((b >= na_ref[0]) | (nv_ref[b] < bm))
            def _():
                copy.start() if go else copy.wait()

        def mark_empty(b, c):
            first = jnp.where(b >= na_ref[0], 0, nv_ref[b])

            def mark(r, c2):
                code_ref[b * bm + r] = EMPTY_SLOT
                return c2
            return lax.fori_loop(first, bm, mark, c)

        n_blocks = xs_ref.shape[0] // bm
        lax.fori_loop(0, n_blocks, lambda b, c: (fill(b, True), c)[1], 0)
        lax.fori_loop(0, n_blocks, mark_empty, 0)
        lax.fori_loop(0, n_blocks, lambda b, c: (fill(b, False), c)[1], 0)

    def row_copy(t, k):
        dst = dest_ref[k * n + i * tm + t]
        return pltpu.make_async_copy(h_ref.at[t], xs_ref.at[dst], sem)

    for t in range(tm):
        for k in range(TOP_K):
            code_ref[dest_ref[k * n + i * tm + t]] = k * n + i * tm + t
            row_copy(t, k).start(priority=(t * TOP_K + k) % 2)
    _split_w1_block(w_ref, g_ref, l_ref, t_ref)
    for t in range(tm):
        for k in range(TOP_K):
            row_copy(t, k).wait()


def _split_w1_block(w_ref, g_ref, l_ref, t_ref):
    _, d, cw = w_ref.shape
    for r in range(d // LANES):
        rows = slice(r * LANES, (r + 1) * LANES)
        t_ref[r] = w_ref[0, rows, :].T
        g_ref[0, rows, :] = t_ref[r, pl.ds(0, cw // 2, stride=2), :].T.astype(BF16)
        l_ref[0, rows, :] = t_ref[r, pl.ds(1, cw // 2, stride=2), :].T.astype(BF16)


def _dispatch(dest_flat, blk_nv, n_act, h2, w1, n_slots):
    n, sub, ln = h2.shape
    e, d, f2 = w1.shape
    tm = TM_ROWS
    steps = n // tm
    per_e = steps // e
    cw = f2 // per_e
    assert per_e * e == steps and cw * per_e == f2 and cw % (2 * LANES) == 0
    w_out = jax.ShapeDtypeStruct((e, d, f2 // 2), BF16)
    slab = lambda width: pl.BlockSpec((1, d, width), lambda i, dest, nv, na: (i // per_e, 0, i % per_e))
    return pl.pallas_call(
        _dispatch_kernel,
        out_shape=[jax.ShapeDtypeStruct((n_slots, sub, ln), h2.dtype),
                   jax.ShapeDtypeStruct((n_slots,), I32), w_out, w_out],
        grid_spec=pltpu.PrefetchScalarGridSpec(
            num_scalar_prefetch=3,
            grid=(steps,),
            in_specs=[pl.BlockSpec((tm, sub, ln), lambda i, dest, nv, na: (i, 0, 0)), slab(cw)],
            out_specs=[pl.BlockSpec(memory_space=pl.ANY), pl.BlockSpec(memory_space=pltpu.SMEM),
                       slab(cw // 2), slab(cw // 2)],
            scratch_shapes=[pltpu.VMEM((BM_FFN, sub, ln), h2.dtype),
                            pltpu.VMEM((d // LANES, cw, LANES), F32),
                            pltpu.SemaphoreType.DMA, pltpu.SemaphoreType.DMA],
        ),
        compiler_params=_cparams(("arbitrary",)),
        name="dispatch",
    )(dest_flat, blk_nv, n_act, h2, w1)


def _ffn_kernel(be_ref, nv_ref, na_ref, code_ref, xs_ref, w1g_ref, w1l_ref, w2_ref, b1g_ref, b1l_ref, b2_ref,
                yt_ref, w2b_ref, stage_ref, sem):
    b = pl.program_id(0)
    steps = pl.num_programs(0)
    bm, sub, ln = xs_ref.shape
    n_rows = yt_ref.shape[0] - 2 * bm
    active = b < na_ref[0]

    def row_copy(step, slot, r, valid):
        code = jnp.where(valid, code_ref[step * bm + r], EMPTY_SLOT)
        dst = jnp.where(code == EMPTY_SLOT, n_rows + slot * bm + r, code)
        return pltpu.make_async_copy(stage_ref.at[slot, r], yt_ref.at[dst], sem.at[slot])

    def scatter(step, slot, go, valid=True):
        for r in range(bm):
            copy = row_copy(step, slot, r, valid)
            copy.start(priority=1) if go else copy.wait()

    @pl.when(b == 0)
    def _():
        stage_ref[...] = jnp.zeros_like(stage_ref)
        spare = [pltpu.make_async_copy(stage_ref.at[p], yt_ref.at[pl.ds(n_rows + p * bm, bm)], sem.at[p])
                 for p in range(2)]
        for copy in spare:
            copy.start()
        for copy in spare:
            copy.wait()

    @pl.when(active & ((b == 0) | (be_ref[b] != be_ref[jnp.maximum(b - 1, 0)])))
    def _():
        w2b_ref[...] = w2_ref[0].astype(BF16)

    @pl.when(active)
    def _():
        prev = jnp.maximum(b - 1, 0)
        other = (b + 1) % 2
        scatter(prev, other, True, valid=b >= 1)
        rows = lax.broadcasted_iota(I32, (bm, 1), 0)
        x = jnp.where(rows < nv_ref[b], xs_ref[...].reshape(bm, sub * ln), 0.0).astype(BF16)
        glu = jnp.dot(x, w1g_ref[0], preferred_element_type=F32) + b1g_ref[0]
        lin = jnp.dot(x, w1l_ref[0], preferred_element_type=F32) + b1l_ref[0]
        glu = jnp.minimum(glu, SWIGLU_LIMIT)
        lin = jnp.clip(lin, -SWIGLU_LIMIT, SWIGLU_LIMIT)
        act = glu * jax.nn.sigmoid(SWIGLU_ALPHA * glu) * (lin + 1.0)
        y = jnp.dot(act.astype(BF16), w2b_ref[...], preferred_element_type=F32) + b2_ref[0]
        stage_ref[b % 2] = y.reshape(bm, sub, ln)
        scatter(prev, other, False, valid=b >= 1)

    @pl.when((b >= 1) & (b == na_ref[0]))
    def _():
        scatter(b - 1, (b - 1) % 2, True)
        scatter(b - 1, (b - 1) % 2, False)

    @pl.when(active & (b == steps - 1))
    def _():
        scatter(b, b % 2, True)
        scatter(b, b % 2, False)


def _ffn(blk_e, blk_nv, n_act, code, xs, w1g, w1l, w2, b1g, b1l, b2, n_rows):
    n_slots, sub, ln = xs.shape
    d = sub * ln
    bm = BM_FFN
    f = w1g.shape[2]
    last = lambda b, na: jnp.maximum(jnp.minimum(b, na[0] - 1), 0)
    blk = lambda b, be, nv, na, code: (last(b, na), 0, 0)
    exp3 = lambda b, be, nv, na, code: (be[last(b, na)], 0, 0)
    return pl.pallas_call(
        _ffn_kernel,
        out_shape=jax.ShapeDtypeStruct((n_rows + 2 * bm, sub, ln), F32),
        grid_spec=pltpu.PrefetchScalarGridSpec(
            num_scalar_prefetch=4,
            grid=(n_slots // bm,),
            in_specs=[pl.BlockSpec((bm, sub, ln), blk),
                      pl.BlockSpec((1, d, f), exp3), pl.BlockSpec((1, d, f), exp3),
                      pl.BlockSpec((1, f, d), exp3),
                      pl.BlockSpec((1, 1, f), exp3), pl.BlockSpec((1, 1, f), exp3),
                      pl.BlockSpec((1, 1, d), exp3)],
            out_specs=pl.BlockSpec(memory_space=pl.ANY),
            scratch_shapes=[pltpu.VMEM((f, d), BF16), pltpu.VMEM((2, bm, sub, ln), F32),
                            pltpu.SemaphoreType.DMA((2,))],
        ),
        compiler_params=_cparams(("arbitrary",)),
        name="ffn",
    )(blk_e, blk_nv, n_act, code, xs, w1g, w1l, w2, b1g, b1l, b2)


def _combine_kernel(*refs):
    y_refs, (gates_ref, x1_ref, mod_ref, gpost_ref, o_ref) = refs[:TOP_K], refs[TOP_K:]
    tm, d = x1_ref.shape
    g = gates_ref[...]
    ffn = g[:, 0:1] * y_refs[0][...].reshape(tm, d)
    for k in range(1, TOP_K):
        ffn = ffn + g[:, k:k + 1] * y_refs[k][...].reshape(tm, d)
    gt2 = mod_ref[0, :, 5 * d:6 * d]
    o_ref[...] = x1_ref[...] + gt2 * _rms(ffn, gpost_ref[...])


def _combine(yt, gates, x1, mod3, g_post, tiles_per_seq):
    n, d = x1.shape
    tm = TM_POST
    tok = lambda w: pl.BlockSpec((tm, w), lambda i: (i, 0))
    y_k = lambda k: pl.BlockSpec((tm,) + yt.shape[1:], lambda i: (k * (n // tm) + i, 0, 0))
    return pl.pallas_call(
        _combine_kernel,
        out_shape=jax.ShapeDtypeStruct((n, d), F32),
        grid=(n // tm,),
        in_specs=[y_k(k) for k in range(TOP_K)]
        + [tok(TOP_K), tok(d), pl.BlockSpec((1, 1, mod3.shape[2]), lambda i: (i // tiles_per_seq, 0, 0)),
           pl.BlockSpec(g_post.shape, lambda i: (0, 0))],
        out_specs=tok(d),
        compiler_params=_cparams(("arbitrary",)),
        name="combine",
    )(*([yt] * TOP_K), gates, x1, mod3, g_post)


def _layer(x, mod, positions, g_pre_mix, g_post_mix, g_pre_ffn, g_post_ffn, w_in, g_q_a, g_kv_a,
           w_q_b, w_kv_b, w_o, w_router, b_router, w_mlp1, b_mlp1, w_mlp2, b_mlp2):
    b, s, d = x.shape
    n = b * s
    mod3 = mod.reshape(b, 1, mod.shape[1])
    row = lambda g: g.reshape(1, -1)

    win, wq, wkv = _relayout_in_weights(w_in, w_q_b, w_kv_b)
    qm, km, vm, qd, kd, vd, qi, kiw = _inproj(
        x, mod3, positions.reshape(b, s, 1), row(g_pre_mix), win, row(g_q_a), row(g_kv_a), wq, wkv,
        _rope_rows())
    y_mla = _mla(qm, km, vm)
    y_dsa = _dsa(qi, kiw, qd, kd, vd, positions.reshape(b, s, 1),
                 positions.reshape(b, s // QB_DSA, 1, QB_DSA))

    wo = w_o.astype(BF16)
    split = MLA_HEADS * MLA_V
    x1, h2, meta, gates = _post(
        y_mla.reshape(n, -1), y_dsa.reshape(n, -1), x.reshape(n, d), mod3, wo[:split], wo[split:],
        row(g_post_mix), row(g_pre_ffn), w_router, row(b_router), s // TM_POST)

    n_slots = n * TOP_K + N_EXPERTS * BM_FFN
    dest, blk = _route(meta, n_slots // BM_FFN)
    dest_flat = dest.reshape(-1)
    xs, code, w1g, w1l = _dispatch(dest_flat, blk[1], blk[2, 0:1], h2, w_mlp1, n_slots)
    f = w_mlp2.shape[1]
    yt = _ffn(blk[0], blk[1], blk[2, 0:1], code, xs, w1g, w1l, w_mlp2,
              b_mlp1[:, 0::2].reshape(-1, 1, f), b_mlp1[:, 1::2].reshape(-1, 1, f),
              b_mlp2.reshape(-1, 1, d), n * TOP_K)
    out = _combine(yt, gates, x1, mod3, row(g_post_ffn), s // TM_POST)
    return out.reshape(b, s, d)


def kernel(x, c, positions, w_ada, b_ada, g_pre_mix, g_post_mix, g_pre_ffn, g_post_ffn, w_in, g_q_a, g_kv_a, w_q_b, w_kv_b, w_o, w_router, b_router, w_mlp1, b_mlp1, w_mlp2, b_mlp2):
    for l in range(w_ada.shape[0]):
        mod = _ada(c, w_ada[l], b_ada[l])
        x = _layer(x, mod, positions, g_pre_mix[l], g_post_mix[l], g_pre_ffn[l], g_post_ffn[l],
                   w_in[l], g_q_a[l], g_kv_a[l], w_q_b[l], w_kv_b[l], w_o[l], w_router[l], b_router[l],
                   w_mlp1[l], b_mlp1[l], w_mlp2[l], b_mlp2[l])
    return x
```
